```python
import jax, jax.numpy as jnp
from jax import lax
import numpy as np

D_MODEL = 1024
BATCH = 4
SEQ = 4096
DEPTH = 2

MIX_WIDTH = D_MODEL
MLSTM_HEADS = 4
MLSTM_WIDTH = MIX_WIDTH // 2
MLSTM_V_DIM = MLSTM_WIDTH // MLSTM_HEADS
MLSTM_QK_DIM = MLSTM_V_DIM // 2
QK_WIDTH = MLSTM_HEADS * MLSTM_QK_DIM
QK_CONV = 4
CHUNK = 64
POOL_WIDTH = MIX_WIDTH - MLSTM_WIDTH
POOL_WINDOWS = (2, 4, 8, 16)
POOL_GROUPS = len(POOL_WINDOWS)
POOL_GROUP_DIM = POOL_WIDTH // POOL_GROUPS
D_FF = 2816
FFN_CONV = 3
EPS = 1e-6
SPLITS = (2 * QK_WIDTH,
          2 * QK_WIDTH + MLSTM_WIDTH,
          2 * QK_WIDTH + 2 * MLSTM_WIDTH,
          2 * QK_WIDTH + 2 * MLSTM_WIDTH + 2 * MLSTM_HEADS)
IN_WIDTH = SPLITS[-1] + POOL_WIDTH

kernel_name = "hymba_mlstm_multiscale_pool_convffn"


def rmsnorm(x, g):
    xf = x.astype(jnp.float32)
    y = xf * lax.rsqrt(jnp.mean(xf * xf, axis=-1, keepdims=True) + EPS)
    return (y * g.astype(jnp.float32)).astype(x.dtype)


def causal_dwconv(x, w):
    K = w.shape[0]
    T = x.shape[1]
    xp = jnp.pad(x, ((0, 0), (K - 1, 0), (0, 0)))
    return sum(xp[:, j:j + T] * w[j] for j in range(K))


def mlstm_chunkwise(q, k, v, i_pre, f_pre):
    B, T, H, dk = q.shape
    dv = v.shape[-1]
    nc = T // CHUNK
    f32 = jnp.float32

    def chunks(a):
        a = a.reshape((B, nc, CHUNK, H) + a.shape[3:])
        return jnp.moveaxis(a, 3, 1)

    qc = chunks(q.astype(f32))
    kc = chunks(k.astype(f32)) * (dk ** -0.5)
    vc = chunks(v.astype(f32))
    ig = chunks(i_pre.astype(f32))
    lf = jax.nn.log_sigmoid(chunks(f_pre.astype(f32)))
    b = jnp.cumsum(lf, axis=-1)
    b_last = b[..., -1]

    a = b_last[..., None] - b + ig
    m_loc = jnp.max(a, axis=-1)
    wgt = jnp.exp(a - m_loc[..., None])
    c_loc = jnp.einsum('bhcs,bhcsk,bhcsv->bhckv', wgt, kc, vc)
    n_loc = jnp.einsum('bhcs,bhcsk->bhck', wgt, kc)

    def step(carry, inp):
        c, n, m = carry
        g, ml, cl, nl = inp
        m_new = jnp.maximum(g + m, ml)
        s_prev = jnp.exp(g + m - m_new)
        s_loc = jnp.exp(ml - m_new)
        c_new = s_prev[..., None, None] * c + s_loc[..., None, None] * cl
        n_new = s_prev[..., None] * n + s_loc[..., None] * nl
        return (c_new, n_new, m_new), (c, n, m)

    init = (jnp.zeros((B, H, dk, dv), f32), jnp.zeros((B, H, dk), f32), jnp.zeros((B, H), f32))
    xs = (jnp.moveaxis(b_last, 2, 0), jnp.moveaxis(m_loc, 2, 0),
          jnp.moveaxis(c_loc, 2, 0), jnp.moveaxis(n_loc, 2, 0))
    _, (c_prev, n_prev, m_prev) = lax.scan(step, init, xs)
    c_prev = jnp.moveaxis(c_prev, 0, 2)
    n_prev = jnp.moveaxis(n_prev, 0, 2)
    m_prev = jnp.moveaxis(m_prev, 0, 2)

    causal = np.tril(np.ones((CHUNK, CHUNK), dtype=bool))
    log_d = jnp.where(causal, b[..., :, None] - b[..., None, :] + ig[..., None, :], -jnp.inf)
    m_inter = b + m_prev[..., None]
    m = jnp.maximum(m_inter, jnp.max(log_d, axis=-1))
    dmat = jnp.exp(log_d - m[..., None])
    s = jnp.einsum('bhcjk,bhcsk->bhcjs', qc, kc) * dmat
    sc = jnp.exp(m_inter - m)
    num = (jnp.einsum('bhcjs,bhcsv->bhcjv', s, vc)
           + sc[..., None] * jnp.einsum('bhcjk,bhckv->bhcjv', qc, c_prev))
    den = jnp.sum(s, axis=-1) + sc * jnp.einsum('bhcjk,bhck->bhcj', qc, n_prev)
    h = num / jnp.maximum(jnp.abs(den), jnp.exp(-m))[..., None]
    h = jnp.moveaxis(h, 1, 3).reshape(B, T, H, dv)
    return h.astype(v.dtype)


def head_rmsnorm(h, g):
    B, T, H, dv = h.shape
    hf = h.astype(jnp.float32)
    y = hf * lax.rsqrt(jnp.mean(hf * hf, axis=-1, keepdims=True) + EPS)
    return (y.reshape(B, T, H * dv) * g.astype(jnp.float32)).astype(h.dtype)


def multiscale_pool(u, w_pool, pool_scale):
    B, T, _ = u.shape
    uf = u.astype(jnp.float32).reshape(B, T, POOL_GROUPS, POOL_GROUP_DIM)
    cs = jnp.pad(jnp.cumsum(uf, axis=1), ((0, 0), (1, 0), (0, 0), (0, 0)))
    pos = jnp.arange(1, T + 1, dtype=jnp.float32)
    outs = []
    for g, w in enumerate(POOL_WINDOWS):
        c = cs[:, :, g]
        lag = jnp.pad(c, ((0, 0), (w, 0), (0, 0)))[:, 1:T + 1]
        mean = (c[:, 1:] - lag) / jnp.minimum(pos, float(w))[None, :, None]
        outs.append(mean - uf[:, :, g])
    d = jnp.stack(outs, axis=2)
    y = jnp.einsum('btgc,gcd->btgd', d, w_pool.astype(jnp.float32)).reshape(B, T, POOL_WIDTH)
    return (y * pool_scale.astype(jnp.float32)).astype(u.dtype)


def hybrid_layer(x, g_mix, w_in, b_gates, w_qk_conv, g_head, w_pool, pool_scale, w_out,
                 g_ffn, w_up, w_ffn_conv, b_ffn_conv, w_down):
    B, T, _ = x.shape
    h = rmsnorm(x, g_mix)
    p = h @ w_in
    qk, v, o, gates, u = jnp.split(p, SPLITS, axis=-1)
    qk = jax.nn.silu(causal_dwconv(qk, w_qk_conv))
    q, k = qk[..., :QK_WIDTH], qk[..., QK_WIDTH:]
    gates = gates + b_gates
    i_pre, f_pre = gates[..., :MLSTM_HEADS], gates[..., MLSTM_HEADS:]
    hm = mlstm_chunkwise(q.reshape(B, T, MLSTM_HEADS, MLSTM_QK_DIM),
                         k.reshape(B, T, MLSTM_HEADS, MLSTM_QK_DIM),
                         v.reshape(B, T, MLSTM_HEADS, MLSTM_V_DIM), i_pre, f_pre)
    hm = head_rmsnorm(hm, g_head) * jax.nn.sigmoid(o)
    hp = multiscale_pool(u, w_pool, pool_scale)
    x = x + jnp.concatenate([hm, hp], axis=-1) @ w_out
    h = rmsnorm(x, g_ffn)
    up = causal_dwconv(h @ w_up, w_ffn_conv) + b_ffn_conv
    gate, val = up[..., :D_FF], up[..., D_FF:]
    return x + (jax.nn.silu(gate) * val) @ w_down


def setup_inputs(seed: int = 0) -> dict:
    key = jax.random.key(seed)
    ks = jax.random.split(key, 16)
    f32 = jnp.float32
    nrm = lambda k, shape, s: (jax.random.normal(k, shape, f32) * s).astype(f32)
    f_bias = jnp.broadcast_to(jnp.linspace(3.0, 6.0, MLSTM_HEADS, dtype=f32), (DEPTH, MLSTM_HEADS))
    b_gates = jnp.concatenate([nrm(ks[3], (DEPTH, MLSTM_HEADS), 0.1),
                               f_bias + nrm(ks[4], (DEPTH, MLSTM_HEADS), 0.1)], axis=-1)
    return {
        "x": nrm(ks[0], (BATCH, SEQ, D_MODEL), 1.0),
        "mix_norm": 1.0 + nrm(ks[1], (DEPTH, D_MODEL), 0.02),
        "w_in": nrm(ks[2], (DEPTH, D_MODEL, IN_WIDTH), D_MODEL ** -0.5),
        "b_gates": b_gates,
        "w_qk_conv": nrm(ks[5], (DEPTH, QK_CONV, 2 * QK_WIDTH), QK_CONV ** -0.5),
        "head_norm": 1.0 + nrm(ks[6], (DEPTH, MLSTM_WIDTH), 0.02),
        "w_pool": nrm(ks[7], (DEPTH, POOL_GROUPS, POOL_GROUP_DIM, POOL_GROUP_DIM), POOL_GROUP_DIM ** -0.5),
        "pool_scale": 1.0 + nrm(ks[8], (DEPTH, POOL_WIDTH), 0.02),
        "w_out": nrm(ks[9], (DEPTH, MIX_WIDTH, D_MODEL), MIX_WIDTH ** -0.5),
        "ffn_norm": 1.0 + nrm(ks[10], (DEPTH, D_MODEL), 0.02),
        "w_up": nrm(ks[11], (DEPTH, D_MODEL, 2 * D_FF), D_MODEL ** -0.5),
        "w_ffn_conv": nrm(ks[12], (DEPTH, FFN_CONV, 2 * D_FF), FFN_CONV ** -0.5),
        "b_ffn_conv": nrm(ks[13], (DEPTH, 2 * D_FF), 0.02),
        "w_down": nrm(ks[14], (DEPTH, D_FF, D_MODEL), D_FF ** -0.5),
        "final_norm": 1.0 + nrm(ks[15], (D_MODEL,), 0.02),
    }


def reference(x, mix_norm, w_in, b_gates, w_qk_conv, head_norm, w_pool, pool_scale, w_out,
              ffn_norm, w_up, w_ffn_conv, b_ffn_conv, w_down, final_norm):
    for l in range(DEPTH):
        x = hybrid_layer(x, mix_norm[l], w_in[l], b_gates[l], w_qk_conv[l], head_norm[l],
                         w_pool[l], pool_scale[l], w_out[l], ffn_norm[l], w_up[l],
                         w_ffn_conv[l], b_ffn_conv[l], w_down[l])
    return rmsnorm(x, final_norm)
```

```python
import functools

import jax
import jax.numpy as jnp
from jax import lax
from jax.experimental import pallas as pl
from jax.experimental.pallas import tpu as pltpu

D_MODEL = 1024
HEADS = 4
DV = 128
DK = 64
QK_WIDTH = HEADS * DK
V_WIDTH = HEADS * DV
QK_CONV = 4
POOL_WINDOWS = (2, 4, 8, 16)
POOL_GROUP = 128
POOL_WIDTH = len(POOL_WINDOWS) * POOL_GROUP
GATES = 2 * HEADS
D_FF = 2816
FFN_CONV = 3
EPS = 1e-6

LANES = 128
SUBLANES = 8
MXU_COLS = 256

MIX_TM = 512
MIX_CHUNK = 128
FFN_TM = 512
FFN_SUB = MXU_COLS
POOL_TAIL = 16
CONV_TAIL = SUBLANES
VMEM_LIMIT = 56 * 1024 * 1024

F32 = jnp.float32
BF16 = jnp.bfloat16


def _dot(a, b):
    return jnp.dot(a, b, preferred_element_type=F32)


def _rms(x, g):
    return x * lax.rsqrt(jnp.mean(x * x, axis=-1, keepdims=True) + EPS) * g


def _silu(x):
    return x * jax.nn.sigmoid(x)


def _shift_rows(ext, j, tail):
    return pltpu.roll(ext, j, axis=0)[tail:, :]


def _mixer_kernel(x_ref, gmix_ref, wmain_ref, wg_ref, bg_ref, wconv_ref, ghead_ref, wpool_ref,
                  pscale_ref, wout_ref, out_ref, qk_tail, u_tail, c_state, m_state, cat_s):
    tm, chunk = MIX_TM, MIX_CHUNK
    t = pl.program_id(1)

    @pl.when(t == 0)
    def _():
        qk_tail[...] = jnp.zeros_like(qk_tail)
        u_tail[...] = jnp.zeros_like(u_tail)
        c_state[...] = jnp.zeros_like(c_state)
        m_state[...] = jnp.zeros_like(m_state)

    x = x_ref[0]
    hb = _rms(x, gmix_ref[...]).astype(BF16)
    p = _dot(hb, wmain_ref[...])
    gates = _dot(hb, wg_ref[...]) + bg_ref[...]

    qk_pre = p[:, :2 * QK_WIDTH]
    ext = jnp.concatenate([qk_tail[...], qk_pre], axis=0)
    qk_tail[...] = qk_pre[tm - CONV_TAIL:, :]
    wc = wconv_ref[...]
    acc = qk_pre * wc[QK_CONV - 1:QK_CONV, :]
    for j in range(1, QK_CONV):
        acc = acc + _shift_rows(ext, j, CONV_TAIL) * wc[QK_CONV - 1 - j:QK_CONV - j, :]
    qk = _silu(acc)
    q = qk[:, :QK_WIDTH].astype(BF16)
    kt = (qk[:, QK_WIDTH:] * (DK ** -0.5)).T
    v = p[:, 2 * QK_WIDTH:2 * QK_WIDTH + V_WIDTH]
    o = p[:, 2 * QK_WIDTH + V_WIDTH:2 * QK_WIDTH + 2 * V_WIDTH]

    lane = lax.broadcasted_iota(jnp.int32, (chunk, LANES), 1)
    is_f = (lane >= HEADS) & (lane < GATES)
    r_i = lax.broadcasted_iota(jnp.int32, (chunk, chunk), 0)
    c_i = lax.broadcasted_iota(jnp.int32, (chunk, chunk), 1)
    causal = r_i >= c_i
    tri = causal.astype(BF16)
    z_parts = []
    for c in range(tm // chunk):
        g_c = gates[c * chunk:(c + 1) * chunk, :]
        lf = jnp.where(is_f, jnp.minimum(g_c, 0.0) - jnp.log(1.0 + jnp.exp(-jnp.abs(g_c))), 0.0)
        hi = lf.astype(BF16)
        r1 = lf - hi.astype(F32)
        mid = r1.astype(BF16)
        lo = (r1 - mid.astype(F32)).astype(BF16)
        b_c = _dot(tri, hi) + _dot(tri, mid) + _dot(tri, lo)
        z_parts.append(jnp.where(is_f, b_c, g_c))
    z = jnp.concatenate(z_parts, axis=0)
    zt = z.T

    ones_col = (lax.broadcasted_iota(jnp.int32, (chunk, LANES), 1) == 0).astype(F32)
    ghead = ghead_ref[...]

    for hd in range(HEADS):
        c_aug = c_state[hd]
        m_prev = m_state[hd:hd + 1, 0:1]
        for c in range(tm // chunk):
            rows = slice(c * chunk, (c + 1) * chunk)
            ig_r = zt[hd:hd + 1, rows]
            b_r = zt[HEADS + hd:HEADS + hd + 1, rows]
            b_c = z[rows, HEADS + hd:HEADS + hd + 1]
            b_last = b_r[:, chunk - 1:chunk]
            q_c = q[rows, hd * DK:(hd + 1) * DK]
            kt_c = kt[hd * DK:(hd + 1) * DK, rows]
            v_aug = jnp.concatenate([v[rows, hd * DV:(hd + 1) * DV], ones_col], axis=1).astype(BF16)

            log_d = b_c + (ig_r - b_r)
            m_intra = jnp.max(jnp.where(causal, log_d, -jnp.inf), axis=1, keepdims=True)
            m_inter = b_c + m_prev
            m = jnp.maximum(m_inter, m_intra)
            dmat = jnp.exp(jnp.where(causal, log_d - m, -jnp.inf))
            s = (_dot(q_c, kt_c.astype(BF16)) * dmat).astype(BF16)
            intra = _dot(s, v_aug)
            inter = _dot(q_c, c_aug.astype(BF16))
            tot = intra + jnp.exp(m_inter - m) * inter
            den = tot[:, DV:DV + 1]
            hh = tot[:, :DV] * (1.0 / jnp.maximum(jnp.abs(den), jnp.exp(-m)))

            cols = slice(hd * DV, (hd + 1) * DV)
            y = hh * lax.rsqrt(jnp.mean(hh * hh, axis=-1, keepdims=True) + EPS) * ghead[:, cols]
            cat_s[rows, cols] = (y * jax.nn.sigmoid(o[rows, cols])).astype(BF16)

            a_r = b_last - b_r + ig_r
            m_loc = jnp.max(a_r, axis=1, keepdims=True)
            ktw = (kt_c * jnp.exp(a_r - m_loc)).astype(BF16)
            c_loc = _dot(ktw, v_aug)
            m_new = jnp.maximum(b_last + m_prev, m_loc)
            c_aug = jnp.exp(b_last + m_prev - m_new) * c_aug + jnp.exp(m_loc - m_new) * c_loc
            m_prev = m_new
        c_state[hd] = c_aug
        m_state[hd:hd + 1, :] = jnp.broadcast_to(m_prev, (1, LANES))

    u = p[:, 2 * QK_WIDTH + 2 * V_WIDTH:]
    extu = jnp.concatenate([u_tail[...], u], axis=0)
    u_tail[...] = u[tm - POOL_TAIL:, :]
    pos = (t * tm + 1 + lax.broadcasted_iota(jnp.int32, (tm, 1), 0)).astype(F32)
    sums = extu
    pscale = pscale_ref[...]
    for g, w in enumerate(POOL_WINDOWS):
        sums = sums + pltpu.roll(sums, w // 2, axis=0)
        mean = sums[POOL_TAIL:, :POOL_GROUP] * (1.0 / jnp.minimum(pos, float(w)))
        d = mean - u[:, g * POOL_GROUP:(g + 1) * POOL_GROUP]
        cols = slice(g * POOL_GROUP, (g + 1) * POOL_GROUP)
        y = _dot(d.astype(BF16), wpool_ref[g]) * pscale[:, cols]
        cat_s[:, V_WIDTH + g * POOL_GROUP:V_WIDTH + (g + 1) * POOL_GROUP] = y.astype(BF16)
        sums = sums[:, POOL_GROUP:]

    out_ref[0] = x + _dot(cat_s[...], wout_ref[...])


def _const_spec(shape):
    nd = len(shape)
    return pl.BlockSpec(shape, lambda b, t: (0,) * nd)


def _mixer(x, gmix, wmain, wg, bg, wconv, ghead, wpool, pscale, wout):
    B, T, D = x.shape
    tm = MIX_TM
    row_spec = pl.BlockSpec((1, tm, D), lambda b, t: (b, t, 0))
    return pl.pallas_call(
        _mixer_kernel,
        grid=(B, T // tm),
        in_specs=[row_spec, _const_spec(gmix.shape), _const_spec(wmain.shape), _const_spec(wg.shape),
                  _const_spec(bg.shape), _const_spec(wconv.shape), _const_spec(ghead.shape),
                  _const_spec(wpool.shape), _const_spec(pscale.shape), _const_spec(wout.shape)],
        out_specs=row_spec,
        out_shape=jax.ShapeDtypeStruct(x.shape, x.dtype),
        scratch_shapes=[
            pltpu.VMEM((CONV_TAIL, 2 * QK_WIDTH), F32),
            pltpu.VMEM((POOL_TAIL, POOL_WIDTH), F32),
            pltpu.VMEM((HEADS, DK, 2 * DV), F32),
            pltpu.VMEM((SUBLANES, LANES), F32),
            pltpu.VMEM((tm, D), BF16),
        ],
        compiler_params=pltpu.CompilerParams(
            dimension_semantics=("arbitrary", "arbitrary"), vmem_limit_bytes=VMEM_LIMIT),
        name="mixer",
    )(x, gmix, wmain, wg, bg, wconv, ghead, wpool, pscale, wout)


def _ffn_kernel(x_ref, gffn_ref, wup_ref, wconv_ref, bconv_ref, wdown_ref, gfinal_ref, out_ref,
                up_tail, act_s, *, final_norm):
    tm = FFN_TM
    t = pl.program_id(1)

    @pl.when(t == 0)
    def _():
        up_tail[...] = jnp.zeros_like(up_tail)

    x = x_ref[0]
    hb = _rms(x, gffn_ref[...]).astype(BF16)

    def conv(cols):
        up = _dot(hb, wup_ref[:, cols])
        ext = jnp.concatenate([up_tail[:, cols], up], axis=0)
        up_tail[:, cols] = up[tm - CONV_TAIL:, :]
        wc = wconv_ref[:, cols]
        acc = up * wc[FFN_CONV - 1:FFN_CONV, :] + bconv_ref[:, cols]
        for j in range(1, FFN_CONV):
            acc = acc + _shift_rows(ext, j, CONV_TAIL) * wc[FFN_CONV - 1 - j:FFN_CONV - j, :]
        return acc

    for f in range(D_FF // FFN_SUB):
        gate = conv(slice(f * FFN_SUB, (f + 1) * FFN_SUB))
        val = conv(slice(D_FF + f * FFN_SUB, D_FF + (f + 1) * FFN_SUB))
        act_s[:, f * FFN_SUB:(f + 1) * FFN_SUB] = (_silu(gate) * val).astype(BF16)

    y = x + _dot(act_s[...], wdown_ref[...])
    if final_norm:
        y = _rms(y, gfinal_ref[...])
    out_ref[0] = y


def _ffn(x, gffn, wup, wconv, bconv, wdown, gfinal, final_norm):
    B, T, D = x.shape
    tm = FFN_TM
    row_spec = pl.BlockSpec((1, tm, D), lambda b, t: (b, t, 0))
    return pl.pallas_call(
        functools.partial(_ffn_kernel, final_norm=final_norm),
        grid=(B, T // tm),
        in_specs=[row_spec, _const_spec(gffn.shape), _const_spec(wup.shape), _const_spec(wconv.shape),
                  _const_spec(bconv.shape), _const_spec(wdown.shape), _const_spec(gfinal.shape)],
        out_specs=row_spec,
        out_shape=jax.ShapeDtypeStruct(x.shape, x.dtype),
        scratch_shapes=[
            pltpu.VMEM((CONV_TAIL, 2 * D_FF), F32),
            pltpu.VMEM((tm, D_FF), BF16),
        ],
        compiler_params=pltpu.CompilerParams(
            dimension_semantics=("arbitrary", "arbitrary"), vmem_limit_bytes=VMEM_LIMIT),
        name="ffn",
    )(x, gffn, wup, wconv, bconv, wdown, gfinal)


def kernel(x, mix_norm, w_in, b_gates, w_qk_conv, head_norm, w_pool, pool_scale, w_out, ffn_norm, w_up,
           w_ffn_conv, b_ffn_conv, w_down, final_norm):
    depth = w_in.shape[0]
    gate_lo = 2 * QK_WIDTH + 2 * V_WIDTH
    row = lambda a: a.reshape(1, -1)
    for l in range(depth):
        wmain = jnp.concatenate([w_in[l, :, :gate_lo], w_in[l, :, gate_lo + GATES:]], axis=1).astype(BF16)
        wg = jnp.pad(w_in[l, :, gate_lo:gate_lo + GATES], ((0, 0), (0, LANES - GATES))).astype(BF16)
        bg = jnp.pad(b_gates[l], (0, LANES - GATES)).reshape(1, LANES)
        x = _mixer(x, row(mix_norm[l]), wmain, wg, bg, w_qk_conv[l], row(head_norm[l]),
                   w_pool[l].astype(BF16), row(pool_scale[l]), w_out[l].astype(BF16))
        x = _ffn(x, row(ffn_norm[l]), w_up[l].astype(BF16), w_ffn_conv[l], row(b_ffn_conv[l]),
                 w_down[l].astype(BF16), row(final_norm), final_norm=(l == depth - 1))
    return x
```

```python
import functools

import jax
import jax.numpy as jnp
from jax import lax
from jax.experimental import pallas as pl
from jax.experimental.pallas import tpu as pltpu

D_MODEL = 1024
HEADS = 4
DV = 128
DK = 64
QK_WIDTH = HEADS * DK
V_WIDTH = HEADS * DV
QK_CONV = 4
POOL_WINDOWS = (2, 4, 8, 16)
POOL_GROUP = 128
POOL_WIDTH = len(POOL_WINDOWS) * POOL_GROUP
GATES = 2 * HEADS
D_FF = 2816
FFN_CONV = 3
EPS = 1e-6

LANES = 128
SUBLANES = 8
MXU_COLS = 256

MIX_TM = 512
MIX_CHUNK = 128
FFN_TM = 512
FFN_SUB = MXU_COLS
POOL_TAIL = 16
CONV_TAIL = SUBLANES
VMEM_LIMIT = 56 * 1024 * 1024

F32 = jnp.float32
BF16 = jnp.bfloat16


def _dot(a, b):
    return jnp.dot(a, b, preferred_element_type=F32)


def _rms(x, g):
    return x * lax.rsqrt(jnp.mean(x * x, axis=-1, keepdims=True) + EPS) * g


def _silu(x):
    return x * jax.nn.sigmoid(x)


def _shift_rows(ext, j, tail):
    return pltpu.roll(ext, j, axis=0)[tail:, :]


def _mixer_kernel(x_ref, gmix_ref, wmain_ref, wg_ref, bg_ref, wconv_ref, ghead_ref, wpool_ref,
                  pscale_ref, wout_ref, out_ref, qk_tail, u_tail, c_state, m_state, cat_s, zt_s):
    tm, chunk = MIX_TM, MIX_CHUNK
    t = pl.program_id(1)

    @pl.when(t == 0)
    def _():
        qk_tail[...] = jnp.zeros_like(qk_tail)
        u_tail[...] = jnp.zeros_like(u_tail)
        c_state[...] = jnp.zeros_like(c_state)
        m_state[...] = jnp.zeros_like(m_state)

    x = x_ref[0]
    hb = _rms(x, gmix_ref[...]).astype(BF16)
    p = _dot(hb, wmain_ref[...])
    gates = _dot(hb, wg_ref[...]) + bg_ref[...]

    qk_pre = p[:, :2 * QK_WIDTH]
    ext = jnp.concatenate([qk_tail[...], qk_pre], axis=0)
    qk_tail[...] = qk_pre[tm - CONV_TAIL:, :]
    wc = wconv_ref[...]
    acc = qk_pre * wc[QK_CONV - 1:QK_CONV, :]
    for j in range(1, QK_CONV):
        acc = acc + _shift_rows(ext, j, CONV_TAIL) * wc[QK_CONV - 1 - j:QK_CONV - j, :]
    qk = _silu(acc)
    q = qk[:, :QK_WIDTH].astype(BF16)
    kt = (qk[:, QK_WIDTH:] * (DK ** -0.5)).T
    v = p[:, 2 * QK_WIDTH:2 * QK_WIDTH + V_WIDTH]
    o = p[:, 2 * QK_WIDTH + V_WIDTH:2 * QK_WIDTH + 2 * V_WIDTH]

    nch = tm // chunk
    lane = lax.broadcasted_iota(jnp.int32, (chunk, LANES), 1)
    is_f = (lane >= HEADS) & (lane < GATES)
    r_i = lax.broadcasted_iota(jnp.int32, (chunk, chunk), 0)
    c_i = lax.broadcasted_iota(jnp.int32, (chunk, chunk), 1)
    causal = r_i >= c_i
    tri = causal.astype(BF16)
    z_parts = []
    for c in range(nch):
        g_c = gates[c * chunk:(c + 1) * chunk, :]
        lf = jnp.where(is_f, jnp.minimum(g_c, 0.0) - jnp.log(1.0 + jnp.exp(-jnp.abs(g_c))), 0.0)
        hi = lf.astype(BF16)
        r1 = lf - hi.astype(F32)
        mid = r1.astype(BF16)
        lo = (r1 - mid.astype(F32)).astype(BF16)
        b_c = _dot(tri, hi) + _dot(tri, mid) + _dot(tri, lo)
        z_parts.append(jnp.where(is_f, b_c, g_c))
    z = jnp.concatenate(z_parts, axis=0)
    zt_s[...] = z.T

    ones_blk = jnp.ones((chunk, DV), BF16)
    ghead = ghead_ref[...]
    pairs = [(c, hd) for c in range(nch) for hd in range(HEADS)]
    rows_of = lambda c: slice(c * chunk, (c + 1) * chunk)

    st = {}
    for c, hd in pairs:
        rows = rows_of(c)
        g_r = zt_s[hd:hd + 1, rows] - zt_s[HEADS + hd:HEADS + hd + 1, rows]
        bc = jnp.broadcast_to(z[rows, HEADS + hd:HEADS + hd + 1], (chunk, LANES))
        log_d = jnp.where(causal, bc + g_r, -jnp.inf)
        m_intra = jnp.broadcast_to(jnp.max(log_d, axis=1, keepdims=True), (chunk, LANES))
        g_max = jnp.broadcast_to(jnp.max(g_r, axis=1, keepdims=True), (1, LANES))
        st[c, hd] = dict(g_r=g_r, bc=bc, log_d=log_d, m_intra=m_intra, g_max=g_max)

    for c, hd in pairs:
        rows, d = rows_of(c), st[c, hd]
        q_c = q[rows, hd * DK:(hd + 1) * DK]
        kt_c = kt[hd * DK:(hd + 1) * DK, rows]
        dmat = jnp.exp(d["log_d"] - d["m_intra"])
        d["s"] = (_dot(q_c, kt_c.astype(BF16)) * dmat).astype(BF16)
        d["ktw"] = (kt_c * jnp.exp(d["g_r"] - d["g_max"])).astype(BF16)
        d["v_aug"] = jnp.concatenate([v[rows, hd * DV:(hd + 1) * DV].astype(BF16), ones_blk], axis=1)
        d["q_c"] = q_c

    for c, hd in pairs:
        d = st[c, hd]
        d["intra"] = _dot(d["s"], d["v_aug"])
        d["c_loc"] = _dot(d["ktw"], d["v_aug"])

    for hd in range(HEADS):
        c_aug = c_state[hd]
        m_prev = m_state[hd:hd + 1, :]
        for c in range(nch):
            d = st[c, hd]
            d["c_prev"], d["m_prev"] = c_aug.astype(BF16), m_prev
            mx = jnp.maximum(m_prev, d["g_max"])
            s_prev, s_loc = jnp.exp(m_prev - mx), jnp.exp(d["g_max"] - mx)
            c_aug = (jnp.concatenate([s_prev, s_prev], axis=1) * c_aug
                     + jnp.concatenate([s_loc, s_loc], axis=1) * d["c_loc"])
            m_prev = d["bc"][chunk - 1:chunk, :] + mx
        c_state[hd] = c_aug
        m_state[hd:hd + 1, :] = m_prev

    for c, hd in pairs:
        d = st[c, hd]
        d["inter"] = _dot(d["q_c"], d["c_prev"])

    for c, hd in pairs:
        rows, d = rows_of(c), st[c, hd]
        m_inter = d["bc"] + d["m_prev"]
        m = jnp.maximum(m_inter, d["m_intra"])
        f_intra, f_inter = jnp.exp(d["m_intra"] - m), jnp.exp(m_inter - m)
        num = f_intra * d["intra"][:, :DV] + f_inter * d["inter"][:, :DV]
        den = f_intra * d["intra"][:, DV:] + f_inter * d["inter"][:, DV:]
        hh = num * (1.0 / jnp.maximum(jnp.abs(den), jnp.exp(-m)))
        cols = slice(hd * DV, (hd + 1) * DV)
        y = hh * lax.rsqrt(jnp.mean(hh * hh, axis=-1, keepdims=True) + EPS) * ghead[:, cols]
        cat_s[rows, cols] = (y * jax.nn.sigmoid(o[rows, cols])).astype(BF16)

    u = p[:, 2 * QK_WIDTH + 2 * V_WIDTH:]
    extu = jnp.concatenate([u_tail[...], u], axis=0)
    u_tail[...] = u[tm - POOL_TAIL:, :]
    pos = (t * tm + 1 + lax.broadcasted_iota(jnp.int32, (tm, 1), 0)).astype(F32)
    sums = extu
    pscale = pscale_ref[...]
    for g, w in enumerate(POOL_WINDOWS):
        sums = sums + pltpu.roll(sums, w // 2, axis=0)
        mean = sums[POOL_TAIL:, :POOL_GROUP] * (1.0 / jnp.minimum(pos, float(w)))
        d = mean - u[:, g * POOL_GROUP:(g + 1) * POOL_GROUP]
        cols = slice(g * POOL_GROUP, (g + 1) * POOL_GROUP)
        y = _dot(d.astype(BF16), wpool_ref[g]) * pscale[:, cols]
        cat_s[:, V_WIDTH + g * POOL_GROUP:V_WIDTH + (g + 1) * POOL_GROUP] = y.astype(BF16)
        sums = sums[:, POOL_GROUP:]

    out_ref[0] = x + _dot(cat_s[...], wout_ref[...])


def _const_spec(shape):
    nd = len(shape)
    return pl.BlockSpec(shape, lambda b, t: (0,) * nd)


def _mixer(x, gmix, wmain, wg, bg, wconv, ghead, wpool, pscale, wout):
    B, T, D = x.shape
    tm = MIX_TM
    row_spec = pl.BlockSpec((1, tm, D), lambda b, t: (b, t, 0))
    return pl.pallas_call(
        _mixer_kernel,
        grid=(B, T // tm),
        in_specs=[row_spec, _const_spec(gmix.shape), _const_spec(wmain.shape), _const_spec(wg.shape),
                  _const_spec(bg.shape), _const_spec(wconv.shape), _const_spec(ghead.shape),
                  _const_spec(wpool.shape), _const_spec(pscale.shape), _const_spec(wout.shape)],
        out_specs=row_spec,
        out_shape=jax.ShapeDtypeStruct(x.shape, x.dtype),
        scratch_shapes=[
            pltpu.VMEM((CONV_TAIL, 2 * QK_WIDTH), F32),
            pltpu.VMEM((POOL_TAIL, POOL_WIDTH), F32),
            pltpu.VMEM((HEADS, DK, 2 * DV), F32),
            pltpu.VMEM((SUBLANES, LANES), F32),
            pltpu.VMEM((tm, D), BF16),
            pltpu.VMEM((LANES, tm), F32),
        ],
        compiler_params=pltpu.CompilerParams(
            dimension_semantics=("arbitrary", "arbitrary"), vmem_limit_bytes=VMEM_LIMIT),
        name="mixer",
    )(x, gmix, wmain, wg, bg, wconv, ghead, wpool, pscale, wout)


def _ffn_kernel(x_ref, gffn_ref, wup_ref, wconv_ref, bconv_ref, wdown_ref, gfinal_ref, out_ref,
                up_tail, act_s, *, final_norm):
    tm = FFN_TM
    t = pl.program_id(1)

    @pl.when(t == 0)
    def _():
        up_tail[...] = jnp.zeros_like(up_tail)

    x = x_ref[0]
    hb = _rms(x, gffn_ref[...]).astype(BF16)

    def conv(cols):
        up = _dot(hb, wup_ref[:, cols])
        ext = jnp.concatenate([up_tail[:, cols], up], axis=0)
        up_tail[:, cols] = up[tm - CONV_TAIL:, :]
        wc = wconv_ref[:, cols]
        acc = up * wc[FFN_CONV - 1:FFN_CONV, :] + bconv_ref[:, cols]
        for j in range(1, FFN_CONV):
            acc = acc + _shift_rows(ext, j, CONV_TAIL) * wc[FFN_CONV - 1 - j:FFN_CONV - j, :]
        return acc

    for f in range(D_FF // FFN_SUB):
        gate = conv(slice(f * FFN_SUB, (f + 1) * FFN_SUB))
        val = conv(slice(D_FF + f * FFN_SUB, D_FF + (f + 1) * FFN_SUB))
        act_s[:, f * FFN_SUB:(f + 1) * FFN_SUB] = (_silu(gate) * val).astype(BF16)

    y = x + _dot(act_s[...], wdown_ref[...])
    if final_norm:
        y = _rms(y, gfinal_ref[...])
    out_ref[0] = y


def _ffn(x, gffn, wup, wconv, bconv, wdown, gfinal, final_norm):
    B, T, D = x.shape
    tm = FFN_TM
    row_spec = pl.BlockSpec((1, tm, D), lambda b, t: (b, t, 0))
    return pl.pallas_call(
        functools.partial(_ffn_kernel, final_norm=final_norm),
        grid=(B, T // tm),
        in_specs=[row_spec, _const_spec(gffn.shape), _const_spec(wup.shape), _const_spec(wconv.shape),
                  _const_spec(bconv.shape), _const_spec(wdown.shape), _const_spec(gfinal.shape)],
        out_specs=row_spec,
        out_shape=jax.ShapeDtypeStruct(x.shape, x.dtype),
        scratch_shapes=[
            pltpu.VMEM((CONV_TAIL, 2 * D_FF), F32),
            pltpu.VMEM((tm, D_FF), BF16),
        ],
        compiler_params=pltpu.CompilerParams(
            dimension_semantics=("arbitrary", "arbitrary"), vmem_limit_bytes=VMEM_LIMIT),
        name="ffn",
    )(x, gffn, wup, wconv, bconv, wdown, gfinal)


def kernel(x, mix_norm, w_in, b_gates, w_qk_conv, head_norm, w_pool, pool_scale, w_out, ffn_norm, w_up,
           w_ffn_conv, b_ffn_conv, w_down, final_norm):
    depth = w_in.shape[0]
    gate_lo = 2 * QK_WIDTH + 2 * V_WIDTH
    row = lambda a: a.reshape(1, -1)
    for l in range(depth):
        wmain = jnp.concatenate([w_in[l, :, :gate_lo], w_in[l, :, gate_lo + GATES:]], axis=1).astype(BF16)
        wg = jnp.pad(w_in[l, :, gate_lo:gate_lo + GATES], ((0, 0), (0, LANES - GATES))).astype(BF16)
        bg = jnp.pad(b_gates[l], (0, LANES - GATES)).reshape(1, LANES)
        x = _mixer(x, row(mix_norm[l]), wmain, wg, bg, w_qk_conv[l], row(head_norm[l]),
                   w_pool[l].astype(BF16), row(pool_scale[l]), w_out[l].astype(BF16))
        x = _ffn(x, row(ffn_norm[l]), w_up[l].astype(BF16), w_ffn_conv[l], row(b_ffn_conv[l]),
                 w_down[l].astype(BF16), row(final_norm), final_norm=(l == depth - 1))
    return x
```

```python
import functools

import jax
import jax.numpy as jnp
from jax import lax
from jax.experimental import pallas as pl
from jax.experimental.pallas import tpu as pltpu

D_MODEL = 1024
HEADS = 4
DV = 128
DK = 64
QK_WIDTH = HEADS * DK
V_WIDTH = HEADS * DV
QK_CONV = 4
POOL_WINDOWS = (2, 4, 8, 16)
POOL_GROUP = 128
POOL_WIDTH = len(POOL_WINDOWS) * POOL_GROUP
GATES = 2 * HEADS
MAIN_WIDTH = 2 * QK_WIDTH + 2 * V_WIDTH + POOL_WIDTH
D_FF = 2816
FFN_CONV = 3
EPS = 1e-6

LANES = 128
SUBLANES = 8
MXU_COLS = 256

MIX_TM = 512
MIX_GROUP = 256
MIX_CHUNK = 128
FFN_TM = 1024
FFN_SUB = MXU_COLS
POOL_TAIL = 16
CONV_TAIL = SUBLANES
VMEM_LIMIT = 56 * 1024 * 1024

F32 = jnp.float32
BF16 = jnp.bfloat16


def _dot(a, b):
    return jnp.dot(a, b, preferred_element_type=F32)


def _rms(x, g):
    return x * lax.rsqrt(jnp.mean(x * x, axis=-1, keepdims=True) + EPS) * g


def _silu(x):
    return x * jax.nn.sigmoid(x)


def _shift_rows(ext, j, tail):
    return pltpu.roll(ext, j, axis=0)[tail:, :]


def _mixer_kernel(x_ref, gmix_ref, wmain_ref, bg_ref, wconv_ref, ghead_ref, wpool_ref,
                  pscale_ref, wout_ref, out_ref, qk_tail, u_tail, c_state, m_state, zt_s):
    tm, grp, chunk = MIX_TM, MIX_GROUP, MIX_CHUNK
    nch, cpg = tm // chunk, grp // chunk
    t = pl.program_id(1)

    @pl.when(t == 0)
    def _():
        qk_tail[...] = jnp.zeros_like(qk_tail)
        u_tail[...] = jnp.zeros_like(u_tail)
        c_state[...] = jnp.zeros_like(c_state)
        m_state[...] = jnp.zeros_like(m_state)

    lane = lax.broadcasted_iota(jnp.int32, (chunk, LANES), 1)
    is_f = (lane >= HEADS) & (lane < GATES)
    causal = (lax.broadcasted_iota(jnp.int32, (chunk, chunk), 0)
              >= lax.broadcasted_iota(jnp.int32, (chunk, chunk), 1))
    tri = causal.astype(BF16)
    wc = wconv_ref[...]
    pscale = pscale_ref[...]
    gmix = gmix_ref[...]

    qk_hist, u_hist = qk_tail[...], u_tail[...]
    fronts = []
    for r in range(tm // grp):
        rows = slice(r * grp, (r + 1) * grp)
        hb = _rms(x_ref[0, rows, :], gmix).astype(BF16)
        p = _dot(hb, wmain_ref[...])
        gates = p[:, MAIN_WIDTH:] + bg_ref[...]

        qk_pre = p[:, :2 * QK_WIDTH]
        ext = jnp.concatenate([qk_hist, qk_pre], axis=0)
        qk_hist = qk_pre[grp - CONV_TAIL:, :]
        acc = qk_pre * wc[QK_CONV - 1:QK_CONV, :]
        for j in range(1, QK_CONV):
            acc = acc + _shift_rows(ext, j, CONV_TAIL) * wc[QK_CONV - 1 - j:QK_CONV - j, :]
        qk = _silu(acc)
        f = dict(q=qk[:, :QK_WIDTH].astype(BF16),
                 kt=(qk[:, QK_WIDTH:] * (DK ** -0.5)).T,
                 v=p[:, 2 * QK_WIDTH:2 * QK_WIDTH + V_WIDTH].astype(BF16),
                 o=p[:, 2 * QK_WIDTH + V_WIDTH:2 * QK_WIDTH + 2 * V_WIDTH])

        z_parts = []
        for c in range(cpg):
            g_c = gates[c * chunk:(c + 1) * chunk, :]
            lf = jnp.where(is_f, jnp.minimum(g_c, 0.0) - jnp.log(1.0 + jnp.exp(-jnp.abs(g_c))), 0.0)
            hi = lf.astype(BF16)
            r1 = lf - hi.astype(F32)
            mid = r1.astype(BF16)
            lo = (r1 - mid.astype(F32)).astype(BF16)
            b3 = _dot(tri, jnp.concatenate([hi, mid, lo], axis=1))
            b_c = b3[:, :LANES] + b3[:, LANES:2 * LANES] + b3[:, 2 * LANES:]
            z_parts.append(jnp.where(is_f, b_c, g_c))
        f["z"] = jnp.concatenate(z_parts, axis=0)
        zt_s[:, rows] = f["z"].T

        u = p[:, 2 * QK_WIDTH + 2 * V_WIDTH:MAIN_WIDTH]
        sums = jnp.concatenate([u_hist, u], axis=0)
        u_hist = u[grp - POOL_TAIL:, :]
        pos = (t * tm + r * grp + 1 + lax.broadcasted_iota(jnp.int32, (grp, 1), 0)).astype(F32)
        ds = []
        for g, w in enumerate(POOL_WINDOWS):
            sums = sums + pltpu.roll(sums, w // 2, axis=0)
            mean = sums[POOL_TAIL:, :POOL_GROUP] * (1.0 / jnp.minimum(pos, float(w)))
            ds.append((mean - u[:, g * POOL_GROUP:(g + 1) * POOL_GROUP]).astype(BF16))
            sums = sums[:, POOL_GROUP:]
        ys = []
        for g2 in range(len(POOL_WINDOWS) // 2):
            cols = slice(2 * g2 * POOL_GROUP, 2 * (g2 + 1) * POOL_GROUP)
            y = _dot(jnp.concatenate(ds[2 * g2:2 * g2 + 2], axis=1), wpool_ref[g2]) * pscale[:, cols]
            ys.append(y.astype(BF16))
        f["pool_out"] = _dot(jnp.concatenate(ys, axis=1), wout_ref[V_WIDTH:, :])
        fronts.append(f)
    qk_tail[...] = qk_hist
    u_tail[...] = u_hist

    ones_blk = jnp.ones((chunk, DV), BF16)
    ghead = ghead_ref[...]
    pairs = [(c, hd) for c in range(nch) for hd in range(HEADS)]
    rows_of = lambda c: slice(c * chunk, (c + 1) * chunk)
    front_of = lambda c: (fronts[c // cpg], rows_of(c % cpg))

    st = {}
    for c, hd in pairs:
        f, lrows = front_of(c)
        rows = rows_of(c)
        g_r = zt_s[hd:hd + 1, rows] - zt_s[HEADS + hd:HEADS + hd + 1, rows]
        bc = jnp.broadcast_to(f["z"][lrows, HEADS + hd:HEADS + hd + 1], (chunk, LANES))
        log_d = jnp.where(causal, bc + g_r, -jnp.inf)
        m_intra = jnp.broadcast_to(jnp.max(log_d, axis=1, keepdims=True), (chunk, LANES))
        g_max = jnp.broadcast_to(jnp.max(g_r, axis=1, keepdims=True), (1, LANES))
        st[c, hd] = dict(g_r=g_r, bc=bc, log_d=log_d, m_intra=m_intra, g_max=g_max)

    for c, hd in pairs:
        f, lrows = front_of(c)
        d = st[c, hd]
        q_c = f["q"][lrows, hd * DK:(hd + 1) * DK]
        kt_c = f["kt"][hd * DK:(hd + 1) * DK, lrows]
        dmat = jnp.exp(d["log_d"] - d["m_intra"])
        d["s"] = _dot(q_c, kt_c.astype(BF16)) * dmat
        d["ktw"] = (kt_c * jnp.exp(d["g_r"] - d["g_max"])).astype(BF16)
        d["v_aug"] = jnp.concatenate([f["v"][lrows, hd * DV:(hd + 1) * DV], ones_blk], axis=1)
        d["q_c"] = q_c

    for c, hd in pairs:
        d = st[c, hd]
        d["c_loc"] = _dot(d["ktw"], d["v_aug"])

    for hd in range(HEADS):
        c_aug = c_state[hd]
        m_prev = m_state[hd:hd + 1, :]
        for c in range(nch):
            d = st[c, hd]
            d["c_prev"], d["m_prev"] = c_aug.astype(BF16), m_prev
            mx = jnp.maximum(m_prev, d["g_max"])
            s_prev, s_loc = jnp.exp(m_prev - mx), jnp.exp(d["g_max"] - mx)
            c_aug = (jnp.concatenate([s_prev, s_prev], axis=1) * c_aug
                     + jnp.concatenate([s_loc, s_loc], axis=1) * d["c_loc"])
            m_prev = d["bc"][chunk - 1:chunk, :] + mx
        c_state[hd] = c_aug
        m_state[hd:hd + 1, :] = m_prev

    k_pad = jnp.zeros((chunk, MXU_COLS - chunk - DK), BF16)
    rhs_pad = jnp.zeros((MXU_COLS - chunk - DK, 2 * DV), BF16)
    for c, hd in pairs:
        d = st[c, hd]
        m_inter = d["bc"] + d["m_prev"]
        m = jnp.maximum(m_inter, d["m_intra"])
        s_w = (jnp.exp(d["m_intra"] - m) * d["s"]).astype(BF16)
        q_w = (jnp.exp(m_inter - m)[:, :DK] * d["q_c"].astype(F32)).astype(BF16)
        lhs = jnp.concatenate([s_w, q_w, k_pad], axis=1)
        rhs = jnp.concatenate([d["v_aug"], d["c_prev"], rhs_pad], axis=0)
        d["tot"], d["m"] = _dot(lhs, rhs), m

    for c in range(nch):
        f, lrows = front_of(c)
        ys = []
        for hd in range(HEADS):
            d = st[c, hd]
            den = d["tot"][:, DV:]
            hh = d["tot"][:, :DV] * (1.0 / jnp.maximum(jnp.abs(den), jnp.exp(-d["m"])))
            cols = slice(hd * DV, (hd + 1) * DV)
            y = hh * lax.rsqrt(jnp.mean(hh * hh, axis=-1, keepdims=True) + EPS) * ghead[:, cols]
            ys.append((y * jax.nn.sigmoid(f["o"][lrows, cols])).astype(BF16))
        rows = rows_of(c)
        mixed = _dot(jnp.concatenate(ys, axis=1), wout_ref[:V_WIDTH, :]) + f["pool_out"][lrows, :]
        out_ref[0, rows, :] = x_ref[0, rows, :] + mixed


def _const_spec(shape):
    nd = len(shape)
    return pl.BlockSpec(shape, lambda b, t: (0,) * nd)


def _resident_spec(shape):
    nd = len(shape)
    return pl.BlockSpec(shape, lambda b, t: (0,) * nd, pipeline_mode=pl.Buffered(1))


def _mixer(x, gmix, wmain, bg, wconv, ghead, wpool, pscale, wout):
    B, T, D = x.shape
    tm = MIX_TM
    row_spec = pl.BlockSpec((1, tm, D), lambda b, t: (b, t, 0))
    return pl.pallas_call(
        _mixer_kernel,
        grid=(B, T // tm),
        in_specs=[row_spec, _const_spec(gmix.shape), _const_spec(wmain.shape), _const_spec(bg.shape), _const_spec(wconv.shape), _const_spec(ghead.shape),
                  _const_spec(wpool.shape), _const_spec(pscale.shape), _const_spec(wout.shape)],
        out_specs=row_spec,
        out_shape=jax.ShapeDtypeStruct(x.shape, x.dtype),
        scratch_shapes=[
            pltpu.VMEM((CONV_TAIL, 2 * QK_WIDTH), F32),
            pltpu.VMEM((POOL_TAIL, POOL_WIDTH), F32),
            pltpu.VMEM((HEADS, DK, 2 * DV), F32),
            pltpu.VMEM((SUBLANES, LANES), F32),
            pltpu.VMEM((LANES, tm), F32),
        ],
        compiler_params=pltpu.CompilerParams(
            dimension_semantics=("arbitrary", "arbitrary"), vmem_limit_bytes=VMEM_LIMIT),
        name="mixer",
    )(x, gmix, wmain, bg, wconv, ghead, wpool, pscale, wout)


def _ffn_kernel(x_ref, gffn_ref, wup_ref, wconv_ref, bconv_ref, wdown_ref, gfinal_ref, out_ref,
                up_tail, act_s, *, final_norm):
    tm = FFN_TM
    t = pl.program_id(1)

    @pl.when(t == 0)
    def _():
        up_tail[...] = jnp.zeros_like(up_tail)

    x = x_ref[0]
    hb = _rms(x, gffn_ref[...]).astype(BF16)

    def conv(cols):
        up = _dot(hb, wup_ref[:, cols])
        ext = jnp.concatenate([up_tail[:, cols], up], axis=0)
        up_tail[:, cols] = up[tm - CONV_TAIL:, :]
        wc = wconv_ref[:, cols]
        acc = up * wc[FFN_CONV - 1:FFN_CONV, :] + bconv_ref[:, cols]
        for j in range(1, FFN_CONV):
            acc = acc + _shift_rows(ext, j, CONV_TAIL) * wc[FFN_CONV - 1 - j:FFN_CONV - j, :]
        return acc

    for f in range(D_FF // FFN_SUB):
        gate = conv(slice(f * FFN_SUB, (f + 1) * FFN_SUB))
        val = conv(slice(D_FF + f * FFN_SUB, D_FF + (f + 1) * FFN_SUB))
        act_s[:, f * FFN_SUB:(f + 1) * FFN_SUB] = (_silu(gate) * val).astype(BF16)

    y = x + _dot(act_s[...], wdown_ref[...])
    if final_norm:
        y = _rms(y, gfinal_ref[...])
    out_ref[0] = y


def _ffn(x, gffn, wup, wconv, bconv, wdown, gfinal, final_norm):
    B, T, D = x.shape
    tm = FFN_TM
    row_spec = pl.BlockSpec((1, tm, D), lambda b, t: (b, t, 0))
    return pl.pallas_call(
        functools.partial(_ffn_kernel, final_norm=final_norm),
        grid=(B, T // tm),
        in_specs=[row_spec, _const_spec(gffn.shape), _resident_spec(wup.shape), _const_spec(wconv.shape),
                  _const_spec(bconv.shape), _resident_spec(wdown.shape), _const_spec(gfinal.shape)],
        out_specs=row_spec,
        out_shape=jax.ShapeDtypeStruct(x.shape, x.dtype),
        scratch_shapes=[
            pltpu.VMEM((CONV_TAIL, 2 * D_FF), F32),
            pltpu.VMEM((tm, D_FF), BF16),
        ],
        compiler_params=pltpu.CompilerParams(
            dimension_semantics=("arbitrary", "arbitrary"), vmem_limit_bytes=VMEM_LIMIT),
        name="ffn",
    )(x, gffn, wup, wconv, bconv, wdown, gfinal)


def kernel(x, mix_norm, w_in, b_gates, w_qk_conv, head_norm, w_pool, pool_scale, w_out, ffn_norm, w_up,
           w_ffn_conv, b_ffn_conv, w_down, final_norm):
    depth = w_in.shape[0]
    gate_lo = 2 * QK_WIDTH + 2 * V_WIDTH
    row = lambda a: a.reshape(1, -1)
    for l in range(depth):
        wmain = jnp.concatenate(
            [w_in[l, :, :gate_lo], w_in[l, :, gate_lo + GATES:], w_in[l, :, gate_lo:gate_lo + GATES],
             jnp.zeros((D_MODEL, LANES - GATES), F32)], axis=1).astype(BF16)
        bg = jnp.pad(b_gates[l], (0, LANES - GATES)).reshape(1, LANES)
        wp = w_pool[l].astype(BF16)
        zero = jnp.zeros_like(wp[0])
        wpool = jnp.stack([jnp.block([[wp[2 * i], zero], [zero, wp[2 * i + 1]]])
                           for i in range(len(POOL_WINDOWS) // 2)])
        x = _mixer(x, row(mix_norm[l]), wmain, bg, w_qk_conv[l], row(head_norm[l]),
                   wpool, row(pool_scale[l]), w_out[l].astype(BF16))
        x = _ffn(x, row(ffn_norm[l]), w_up[l].astype(BF16), w_ffn_conv[l], row(b_ffn_conv[l]),
                 w_down[l].astype(BF16), row(final_norm), final_norm=(l == depth - 1))
    return x
```

```python
import functools

import jax
import jax.numpy as jnp
from jax import lax
from jax.experimental import pallas as pl
from jax.experimental.pallas import tpu as pltpu

D_MODEL = 1024
HEADS = 4
DV = 128
DK = 64
QK_WIDTH = HEADS * DK
V_WIDTH = HEADS * DV
QK_CONV = 4
POOL_WINDOWS = (2, 4, 8, 16)
POOL_GROUP = 128
POOL_WIDTH = len(POOL_WINDOWS) * POOL_GROUP
GATES = 2 * HEADS
MAIN_WIDTH = 2 * QK_WIDTH + 2 * V_WIDTH + POOL_WIDTH
D_FF = 2816
FFN_CONV = 3
EPS = 1e-6

LANES = 128
SUBLANES = 8
MXU_COLS = 256

MIX_TM = 512
MIX_GROUP = 256
MIX_CHUNK = 128
FFN_TM = 1024
FFN_SUB = MXU_COLS
POOL_TAIL = 16
CONV_TAIL = SUBLANES
VMEM_LIMIT = 56 * 1024 * 1024

F32 = jnp.float32
BF16 = jnp.bfloat16


def _dot(a, b):
    return jnp.dot(a, b, preferred_element_type=F32)


def _rms(x, g):
    return x * lax.rsqrt(jnp.mean(x * x, axis=-1, keepdims=True) + EPS) * g


def _silu(x):
    return x * jax.nn.sigmoid(x)


def _shift_rows(ext, j, tail):
    return pltpu.roll(ext, j, axis=0)[tail:, :]


def _mixer_kernel(x_ref, gmix_ref, wmain_ref, bg_ref, wconv_ref, ghead_ref, wpool_ref,
                  pscale_ref, wout_ref, out_ref, qk_tail, u_tail, c_state, m_state, zt_s):
    tm, grp, chunk = MIX_TM, MIX_GROUP, MIX_CHUNK
    nch, cpg = tm // chunk, grp // chunk
    t = pl.program_id(1)

    @pl.when(t == 0)
    def _():
        qk_tail[...] = jnp.zeros_like(qk_tail)
        u_tail[...] = jnp.zeros_like(u_tail)
        c_state[...] = jnp.zeros_like(c_state)
        m_state[...] = jnp.zeros_like(m_state)

    lane = lax.broadcasted_iota(jnp.int32, (chunk, LANES), 1)
    is_f = (lane >= HEADS) & (lane < GATES)
    causal = (lax.broadcasted_iota(jnp.int32, (chunk, chunk), 0)
              >= lax.broadcasted_iota(jnp.int32, (chunk, chunk), 1))
    tri = causal.astype(BF16)
    wc = wconv_ref[...]
    pscale = pscale_ref[...]
    gmix = gmix_ref[...]

    qk_hist, u_hist = qk_tail[...], u_tail[...]
    fronts = []
    for r in range(tm // grp):
        rows = slice(r * grp, (r + 1) * grp)
        hb = _rms(x_ref[0, rows, :], gmix).astype(BF16)
        p = _dot(hb, wmain_ref[...])
        gates = p[:, MAIN_WIDTH:] + bg_ref[...]

        qk_pre = p[:, :2 * QK_WIDTH]
        ext = jnp.concatenate([qk_hist, qk_pre], axis=0)
        qk_hist = qk_pre[grp - CONV_TAIL:, :]
        acc = qk_pre * wc[QK_CONV - 1:QK_CONV, :]
        for j in range(1, QK_CONV):
            acc = acc + _shift_rows(ext, j, CONV_TAIL) * wc[QK_CONV - 1 - j:QK_CONV - j, :]
        qk = _silu(acc)
        f = dict(q=qk[:, :QK_WIDTH].astype(BF16),
                 kt=(qk[:, QK_WIDTH:] * (DK ** -0.5)).T,
                 v=p[:, 2 * QK_WIDTH:2 * QK_WIDTH + V_WIDTH].astype(BF16),
                 o=p[:, 2 * QK_WIDTH + V_WIDTH:2 * QK_WIDTH + 2 * V_WIDTH])

        z_parts = []
        for c in range(cpg):
            g_c = gates[c * chunk:(c + 1) * chunk, :]
            lf = jnp.where(is_f, jnp.minimum(g_c, 0.0) - jnp.log(1.0 + jnp.exp(-jnp.abs(g_c))), 0.0)
            hi = lf.astype(BF16)
            r1 = lf - hi.astype(F32)
            mid = r1.astype(BF16)
            lo = (r1 - mid.astype(F32)).astype(BF16)
            b3 = _dot(tri, jnp.concatenate([hi, mid, lo], axis=1))
            b_c = b3[:, :LANES] + b3[:, LANES:2 * LANES] + b3[:, 2 * LANES:]
            z_parts.append(jnp.where(is_f, b_c, g_c))
        f["z"] = jnp.concatenate(z_parts, axis=0)
        zt_s[:, rows] = f["z"].T

        u = p[:, 2 * QK_WIDTH + 2 * V_WIDTH:MAIN_WIDTH]
        sums = jnp.concatenate([u_hist, u], axis=0)
        u_hist = u[grp - POOL_TAIL:, :]
        pos = (t * tm + r * grp + 1 + lax.broadcasted_iota(jnp.int32, (grp, 1), 0)).astype(F32)
        ds = []
        for g, w in enumerate(POOL_WINDOWS):
            sums = sums + pltpu.roll(sums, w // 2, axis=0)
            mean = sums[POOL_TAIL:, :POOL_GROUP] * (1.0 / jnp.minimum(pos, float(w)))
            ds.append((mean - u[:, g * POOL_GROUP:(g + 1) * POOL_GROUP]).astype(BF16))
            sums = sums[:, POOL_GROUP:]
        ys = []
        for g2 in range(len(POOL_WINDOWS) // 2):
            cols = slice(2 * g2 * POOL_GROUP, 2 * (g2 + 1) * POOL_GROUP)
            y = _dot(jnp.concatenate(ds[2 * g2:2 * g2 + 2], axis=1), wpool_ref[g2]) * pscale[:, cols]
            ys.append(y.astype(BF16))
        f["pool_out"] = _dot(jnp.concatenate(ys, axis=1), wout_ref[V_WIDTH:, :])
        fronts.append(f)
    qk_tail[...] = qk_hist
    u_tail[...] = u_hist

    ones_blk = jnp.ones((chunk, DV), BF16)
    ghead = ghead_ref[...]
    pairs = [(c, hd) for c in range(nch) for hd in range(HEADS)]
    rows_of = lambda c: slice(c * chunk, (c + 1) * chunk)
    front_of = lambda c: (fronts[c // cpg], rows_of(c % cpg))

    st = {}
    for c, hd in pairs:
        f, lrows = front_of(c)
        rows = rows_of(c)
        g_r = zt_s[hd:hd + 1, rows] - zt_s[HEADS + hd:HEADS + hd + 1, rows]
        bc = jnp.broadcast_to(f["z"][lrows, HEADS + hd:HEADS + hd + 1], (chunk, LANES))
        log_d = jnp.where(causal, bc + g_r, -jnp.inf)
        m_intra = jnp.broadcast_to(jnp.max(log_d, axis=1, keepdims=True), (chunk, LANES))
        g_max = jnp.broadcast_to(jnp.max(g_r, axis=1, keepdims=True), (1, LANES))
        st[c, hd] = dict(g_r=g_r, bc=bc, log_d=log_d, m_intra=m_intra, g_max=g_max)

    for c, hd in pairs:
        f, lrows = front_of(c)
        d = st[c, hd]
        q_c = f["q"][lrows, hd * DK:(hd + 1) * DK]
        kt_c = f["kt"][hd * DK:(hd + 1) * DK, lrows]
        dmat = jnp.exp(d["log_d"] - d["m_intra"])
        d["s"] = _dot(q_c, kt_c.astype(BF16)) * dmat
        d["ktw"] = (kt_c * jnp.exp(d["g_r"] - d["g_max"])).astype(BF16)
        d["v_aug"] = jnp.concatenate([f["v"][lrows, hd * DV:(hd + 1) * DV], ones_blk], axis=1)
        d["q_c"] = q_c

    for c, hd in pairs:
        d = st[c, hd]
        d["c_loc"] = _dot(d["ktw"], d["v_aug"])

    for hd in range(HEADS):
        c_aug = c_state[hd]
        m_prev = m_state[hd:hd + 1, :]
        for c in range(nch):
            d = st[c, hd]
            d["c_prev"], d["m_prev"] = c_aug.astype(BF16), m_prev
            mx = jnp.maximum(m_prev, d["g_max"])
            s_prev, s_loc = jnp.exp(m_prev - mx), jnp.exp(d["g_max"] - mx)
            c_aug = (jnp.concatenate([s_prev, s_prev], axis=1) * c_aug
                     + jnp.concatenate([s_loc, s_loc], axis=1) * d["c_loc"])
            m_prev = d["bc"][chunk - 1:chunk, :] + mx
        c_state[hd] = c_aug
        m_state[hd:hd + 1, :] = m_prev

    k_pad = jnp.zeros((chunk, MXU_COLS - chunk - DK), BF16)
    rhs_pad = jnp.zeros((MXU_COLS - chunk - DK, 2 * DV), BF16)
    for c, hd in pairs:
        d = st[c, hd]
        m_inter = d["bc"] + d["m_prev"]
        m = jnp.maximum(m_inter, d["m_intra"])
        s_w = (jnp.exp(d["m_intra"] - m) * d["s"]).astype(BF16)
        q_w = (jnp.exp(m_inter - m)[:, :DK] * d["q_c"].astype(F32)).astype(BF16)
        lhs = jnp.concatenate([s_w, q_w, k_pad], axis=1)
        rhs = jnp.concatenate([d["v_aug"], d["c_prev"], rhs_pad], axis=0)
        d["tot"], d["m"] = _dot(lhs, rhs), m

    for c in range(nch):
        f, lrows = front_of(c)
        ys = []
        for hd in range(HEADS):
            d = st[c, hd]
            den = d["tot"][:, DV:]
            hh = d["tot"][:, :DV] * (1.0 / jnp.maximum(jnp.abs(den), jnp.exp(-d["m"])))
            cols = slice(hd * DV, (hd + 1) * DV)
            y = hh * lax.rsqrt(jnp.mean(hh * hh, axis=-1, keepdims=True) + EPS) * ghead[:, cols]
            ys.append((y * jax.nn.sigmoid(f["o"][lrows, cols])).astype(BF16))
        rows = rows_of(c)
        mixed = _dot(jnp.concatenate(ys, axis=1), wout_ref[:V_WIDTH, :]) + f["pool_out"][lrows, :]
        out_ref[0, rows, :] = x_ref[0, rows, :] + mixed


def _layer_spec(stacked, layer, buffers=None):
    nd = stacked.ndim - 1
    kw = {} if buffers is None else dict(pipeline_mode=pl.Buffered(buffers))
    return pl.BlockSpec((None,) + stacked.shape[1:], lambda b, t: (layer,) + (0,) * nd, **kw)


def _mixer(x, layer, gmix, wmain, bg, wconv, ghead, wpool, pscale, wout):
    B, T, D = x.shape
    tm = MIX_TM
    row_spec = pl.BlockSpec((1, tm, D), lambda b, t: (b, t, 0))
    params = (gmix, wmain, bg, wconv, ghead, wpool, pscale, wout)
    return pl.pallas_call(
        _mixer_kernel,
        grid=(B, T // tm),
        in_specs=[row_spec] + [_layer_spec(a, layer) for a in params],
        out_specs=row_spec,
        out_shape=jax.ShapeDtypeStruct(x.shape, x.dtype),
        scratch_shapes=[
            pltpu.VMEM((CONV_TAIL, 2 * QK_WIDTH), F32),
            pltpu.VMEM((POOL_TAIL, POOL_WIDTH), F32),
            pltpu.VMEM((HEADS, DK, 2 * DV), F32),
            pltpu.VMEM((SUBLANES, LANES), F32),
            pltpu.VMEM((LANES, tm), F32),
        ],
        compiler_params=pltpu.CompilerParams(
            dimension_semantics=("arbitrary", "arbitrary"), vmem_limit_bytes=VMEM_LIMIT),
        name="mixer",
    )(x, *params)


def _ffn_kernel(x_ref, gffn_ref, wup_ref, wconv_ref, bconv_ref, wdown_ref, gfinal_ref, out_ref,
                up_tail, act_s, hb_s, *, final_norm):
    tm = FFN_TM
    t = pl.program_id(1)

    @pl.when(t == 0)
    def _():
        up_tail[...] = jnp.zeros_like(up_tail)

    hb_s[...] = _rms(x_ref[0], gffn_ref[...]).astype(BF16)

    def conv(cols, scale=None):
        up = _dot(hb_s[...], wup_ref[:, cols])
        ext = jnp.concatenate([up_tail[:, cols], up], axis=0)
        up_tail[:, cols] = up[tm - CONV_TAIL:, :]
        wc, bias = wconv_ref[:, cols], bconv_ref[:, cols]
        if scale is not None:
            wc, bias = wc * scale, bias * scale
        acc = up * wc[FFN_CONV - 1:FFN_CONV, :] + bias
        for j in range(1, FFN_CONV):
            acc = acc + _shift_rows(ext, j, CONV_TAIL) * wc[FFN_CONV - 1 - j:FFN_CONV - j, :]
        return acc

    for f in range(D_FF // FFN_SUB):
        half_gate = conv(slice(f * FFN_SUB, (f + 1) * FFN_SUB), 0.5)
        val = conv(slice(D_FF + f * FFN_SUB, D_FF + (f + 1) * FFN_SUB))
        act = half_gate * (1.0 + jnp.tanh(half_gate)) * val
        act_s[:, f * FFN_SUB:(f + 1) * FFN_SUB] = act.astype(BF16)

    y = x_ref[0] + _dot(act_s[...], wdown_ref[...])
    if final_norm:
        y = _rms(y, gfinal_ref[...])
    out_ref[0] = y


def _ffn(x, layer, gffn, wup, wconv, bconv, wdown, gfinal, final_norm):
    B, T, D = x.shape
    tm = FFN_TM
    row_spec = pl.BlockSpec((1, tm, D), lambda b, t: (b, t, 0))
    return pl.pallas_call(
        functools.partial(_ffn_kernel, final_norm=final_norm),
        grid=(B, T // tm),
        in_specs=[row_spec, _layer_spec(gffn, layer), _layer_spec(wup, layer, 1), _layer_spec(wconv, layer),
                  _layer_spec(bconv, layer), _layer_spec(wdown, layer, 1), _layer_spec(gfinal, 0)],
        out_specs=row_spec,
        out_shape=jax.ShapeDtypeStruct(x.shape, x.dtype),
        scratch_shapes=[
            pltpu.VMEM((CONV_TAIL, 2 * D_FF), F32),
            pltpu.VMEM((tm, D_FF), BF16),
            pltpu.VMEM((tm, D), BF16),
        ],
        compiler_params=pltpu.CompilerParams(
            dimension_semantics=("arbitrary", "arbitrary"), vmem_limit_bytes=VMEM_LIMIT),
        name="ffn",
    )(x, gffn, wup, wconv, bconv, wdown, gfinal)


def kernel(x, mix_norm, w_in, b_gates, w_qk_conv, head_norm, w_pool, pool_scale, w_out, ffn_norm, w_up,
           w_ffn_conv, b_ffn_conv, w_down, final_norm):
    depth = w_in.shape[0]
    gate_lo = 2 * QK_WIDTH + 2 * V_WIDTH
    rows = lambda a: a[:, None, :]
    wmain = jnp.concatenate(
        [w_in[:, :, :gate_lo], w_in[:, :, gate_lo + GATES:], w_in[:, :, gate_lo:gate_lo + GATES],
         jnp.zeros((depth, D_MODEL, LANES - GATES), F32)], axis=2).astype(BF16)
    bg = rows(jnp.pad(b_gates, ((0, 0), (0, LANES - GATES))))
    wp = w_pool.astype(BF16).reshape(depth, len(POOL_WINDOWS) // 2, 2, POOL_GROUP, POOL_GROUP)
    zero = jnp.zeros_like(wp[:, :, 0])
    wpool = jnp.concatenate([jnp.concatenate([wp[:, :, 0], zero], axis=-1),
                             jnp.concatenate([zero, wp[:, :, 1]], axis=-1)], axis=-2)
    wout, wup, wdown = w_out.astype(BF16), w_up.astype(BF16), w_down.astype(BF16)
    gmix, ghead, pscale, gffn, bconv = map(rows, (mix_norm, head_norm, pool_scale, ffn_norm, b_ffn_conv))
    gfinal = final_norm.reshape(1, 1, -1)
    for l in range(depth):
        x = _mixer(x, l, gmix, wmain, bg, w_qk_conv, ghead, wpool, pscale, wout)
        x = _ffn(x, l, gffn, wup, w_ffn_conv, bconv, wdown, gfinal, final_norm=(l == depth - 1))
    return x
```

```python
import functools

import jax
import jax.numpy as jnp
from jax import lax
from jax.experimental import pallas as pl
from jax.experimental.pallas import tpu as pltpu

D_MODEL = 1024
HEADS = 4
DV = 128
DK = 64
QK_WIDTH = HEADS * DK
V_WIDTH = HEADS * DV
QK_CONV = 4
POOL_WINDOWS = (2, 4, 8, 16)
POOL_GROUP = 128
POOL_WIDTH = len(POOL_WINDOWS) * POOL_GROUP
GATES = 2 * HEADS
MAIN_WIDTH = 2 * QK_WIDTH + 2 * V_WIDTH + POOL_WIDTH
D_FF = 2816
FFN_CONV = 3
EPS = 1e-6

LANES = 128
SUBLANES = 8
MXU_COLS = 256

MIX_TM = 1024
MIX_GROUP = 256
MIX_CHUNK = 128
FFN_TM = 1024
FFN_SUB = MXU_COLS
POOL_TAIL = 16
CONV_TAIL = SUBLANES
VMEM_LIMIT = 56 * 1024 * 1024

F32 = jnp.float32
BF16 = jnp.bfloat16


def _dot(a, b):
    return jnp.dot(a, b, preferred_element_type=F32)


def _rms(x, g):
    return x * lax.rsqrt(jnp.mean(x * x, axis=-1, keepdims=True) + EPS) * g


def _silu(x):
    return x * jax.nn.sigmoid(x)


def _shift_rows(ext, j, tail):
    return pltpu.roll(ext, j, axis=0)[tail:, :]


def _mixer_kernel(x_ref, gmix_ref, wmain_ref, bg_ref, wconv_ref, ghead_ref, wpool_ref,
                  pscale_ref, wout_ref, out_ref, qk_tail, u_tail, c_state, m_state, zt_s):
    tm, grp, chunk = MIX_TM, MIX_GROUP, MIX_CHUNK
    ngrp, cpg = tm // grp, grp // chunk
    t = pl.program_id(1)

    @pl.when(t == 0)
    def _():
        qk_tail[...] = jnp.zeros_like(qk_tail)
        u_tail[...] = jnp.zeros_like(u_tail)
        c_state[...] = jnp.zeros_like(c_state)
        m_state[...] = jnp.zeros_like(m_state)

    lane = lax.broadcasted_iota(jnp.int32, (chunk, LANES), 1)
    is_f = (lane >= HEADS) & (lane < GATES)
    causal = (lax.broadcasted_iota(jnp.int32, (chunk, chunk), 0)
              >= lax.broadcasted_iota(jnp.int32, (chunk, chunk), 1))
    tri = causal.astype(BF16)
    wc = wconv_ref[...]
    pscale = pscale_ref[...]
    gmix = gmix_ref[...]
    ghead = ghead_ref[...]
    ones_blk = jnp.ones((chunk, DV), BF16)
    k_pad = jnp.zeros((chunk, MXU_COLS - chunk - DK), BF16)
    rhs_pad = jnp.zeros((MXU_COLS - chunk - DK, 2 * DV), BF16)
    rows_of = lambda c: slice(c * chunk, (c + 1) * chunk)

    def front(r, qk_hist, u_hist):
        rows = slice(r * grp, (r + 1) * grp)
        hb = _rms(x_ref[0, rows, :], gmix).astype(BF16)
        p = _dot(hb, wmain_ref[...])
        gates = p[:, MAIN_WIDTH:] + bg_ref[...]

        qk_pre = p[:, :2 * QK_WIDTH]
        ext = jnp.concatenate([qk_hist, qk_pre], axis=0)
        acc = qk_pre * wc[QK_CONV - 1:QK_CONV, :]
        for j in range(1, QK_CONV):
            acc = acc + _shift_rows(ext, j, CONV_TAIL) * wc[QK_CONV - 1 - j:QK_CONV - j, :]
        qk = _silu(acc)
        f = dict(q=qk[:, :QK_WIDTH].astype(BF16),
                 kt=(qk[:, QK_WIDTH:] * (DK ** -0.5)).T,
                 v=p[:, 2 * QK_WIDTH:2 * QK_WIDTH + V_WIDTH].astype(BF16),
                 o=p[:, 2 * QK_WIDTH + V_WIDTH:2 * QK_WIDTH + 2 * V_WIDTH])

        z_parts = []
        for c in range(cpg):
            g_c = gates[rows_of(c), :]
            lf = jnp.where(is_f, jnp.minimum(g_c, 0.0) - jnp.log(1.0 + jnp.exp(-jnp.abs(g_c))), 0.0)
            hi = lf.astype(BF16)
            r1 = lf - hi.astype(F32)
            mid = r1.astype(BF16)
            lo = (r1 - mid.astype(F32)).astype(BF16)
            b3 = _dot(tri, jnp.concatenate([hi, mid, lo], axis=1))
            b_c = b3[:, :LANES] + b3[:, LANES:2 * LANES] + b3[:, 2 * LANES:]
            z_parts.append(jnp.where(is_f, b_c, g_c))
        f["z"] = jnp.concatenate(z_parts, axis=0)
        zt_s[:, rows] = f["z"].T
        f["g_r"] = {}
        for c in range(cpg):
            lanes = slice(r * grp + c * chunk, r * grp + (c + 1) * chunk)
            for hd in range(HEADS):
                f["g_r"][c, hd] = zt_s[hd:hd + 1, lanes] - zt_s[HEADS + hd:HEADS + hd + 1, lanes]

        u = p[:, 2 * QK_WIDTH + 2 * V_WIDTH:MAIN_WIDTH]
        sums = jnp.concatenate([u_hist, u], axis=0)
        pos = (t * tm + r * grp + 1 + lax.broadcasted_iota(jnp.int32, (grp, 1), 0)).astype(F32)
        ds = []
        for g, w in enumerate(POOL_WINDOWS):
            sums = sums + pltpu.roll(sums, w // 2, axis=0)
            mean = sums[POOL_TAIL:, :POOL_GROUP] * (1.0 / jnp.minimum(pos, float(w)))
            ds.append((mean - u[:, g * POOL_GROUP:(g + 1) * POOL_GROUP]).astype(BF16))
            sums = sums[:, POOL_GROUP:]
        ys = []
        for g2 in range(len(POOL_WINDOWS) // 2):
            cols = slice(2 * g2 * POOL_GROUP, 2 * (g2 + 1) * POOL_GROUP)
            y = _dot(jnp.concatenate(ds[2 * g2:2 * g2 + 2], axis=1), wpool_ref[g2]) * pscale[:, cols]
            ys.append(y.astype(BF16))
        f["pool_out"] = _dot(jnp.concatenate(ys, axis=1), wout_ref[V_WIDTH:, :])
        return f, qk_pre[grp - CONV_TAIL:, :], u[grp - POOL_TAIL:, :]

    def mlstm(r, f, states):
        pairs = [(c, hd) for c in range(cpg) for hd in range(HEADS)]

        st = {}
        for c, hd in pairs:
            g_r = f["g_r"][c, hd]
            bc =jnp.broadcast_to(f["z"][rows_of(c), HEADS + hd:HEADS + hd + 1], (chunk, LANES))
            log_d = jnp.where(causal, bc + g_r, -jnp.inf)
            m_intra = jnp.broadcast_to(jnp.max(log_d, axis=1, keepdims=True), (chunk, LANES))
            g_max = jnp.broadcast_to(jnp.max(g_r, axis=1, keepdims=True), (1, LANES))
            st[c, hd] = dict(g_r=g_r, bc=bc, log_d=log_d, m_intra=m_intra, g_max=g_max)

        for c, hd in pairs:
            d = st[c, hd]
            q_c = f["q"][rows_of(c), hd * DK:(hd + 1) * DK]
            kt_c = f["kt"][hd * DK:(hd + 1) * DK, rows_of(c)]
            dmat = jnp.exp(d["log_d"] - d["m_intra"])
            d["s"] = _dot(q_c, kt_c.astype(BF16)) * dmat
            d["ktw"] = (kt_c * jnp.exp(d["g_r"] - d["g_max"])).astype(BF16)
            d["v_aug"] = jnp.concatenate([f["v"][rows_of(c), hd * DV:(hd + 1) * DV], ones_blk], axis=1)
            d["q_c"] = q_c

        for c, hd in pairs:
            d = st[c, hd]
            d["c_loc"] = _dot(d["ktw"], d["v_aug"])

        new_states = []
        for hd in range(HEADS):
            c_aug, m_prev = states[hd]
            for c in range(cpg):
                d = st[c, hd]
                d["c_prev"], d["m_prev"] = c_aug.astype(BF16), m_prev
                mx = jnp.maximum(m_prev, d["g_max"])
                s_prev, s_loc = jnp.exp(m_prev - mx), jnp.exp(d["g_max"] - mx)
                c_aug = (jnp.concatenate([s_prev, s_prev], axis=1) * c_aug
                         + jnp.concatenate([s_loc, s_loc], axis=1) * d["c_loc"])
                m_prev = d["bc"][chunk - 1:chunk, :] + mx
            new_states.append((c_aug, m_prev))

        for c, hd in pairs:
            d = st[c, hd]
            m_inter = d["bc"] + d["m_prev"]
            m = jnp.maximum(m_inter, d["m_intra"])
            s_w = (jnp.exp(d["m_intra"] - m) * d["s"]).astype(BF16)
            q_w = (jnp.exp(m_inter - m)[:, :DK] * d["q_c"].astype(F32)).astype(BF16)
            lhs = jnp.concatenate([s_w, q_w, k_pad], axis=1)
            rhs = jnp.concatenate([d["v_aug"], d["c_prev"], rhs_pad], axis=0)
            d["tot"], d["m"] = _dot(lhs, rhs), m

        for c in range(cpg):
            ys = []
            for hd in range(HEADS):
                d = st[c, hd]
                den = d["tot"][:, DV:]
                hh = d["tot"][:, :DV] * (1.0 / jnp.maximum(jnp.abs(den), jnp.exp(-d["m"])))
                cols = slice(hd * DV, (hd + 1) * DV)
                y = hh * lax.rsqrt(jnp.mean(hh * hh, axis=-1, keepdims=True) + EPS) * ghead[:, cols]
                ys.append((y * jax.nn.sigmoid(f["o"][rows_of(c), cols])).astype(BF16))
            rows = slice(r * grp + c * chunk, r * grp + (c + 1) * chunk)
            mixed = _dot(jnp.concatenate(ys, axis=1), wout_ref[:V_WIDTH, :]) + f["pool_out"][rows_of(c), :]
            out_ref[0, rows, :] = x_ref[0, rows, :] + mixed
        return new_states

    states = [(c_state[hd], m_state[hd:hd + 1, :]) for hd in range(HEADS)]
    nxt = front(0, qk_tail[...], u_tail[...])
    for r in range(ngrp):
        f, qk_hist, u_hist = nxt
        if r + 1 < ngrp:
            nxt = front(r + 1, qk_hist, u_hist)
        states = mlstm(r, f, states)
    qk_tail[...] = nxt[1]
    u_tail[...] = nxt[2]
    for hd in range(HEADS):
        c_state[hd] = states[hd][0]
        m_state[hd:hd + 1, :] = states[hd][1]


def _layer_spec(stacked, layer, buffers=None):
    nd = stacked.ndim - 1
    kw = {} if buffers is None else dict(pipeline_mode=pl.Buffered(buffers))
    return pl.BlockSpec((None,) + stacked.shape[1:], lambda b, t: (layer,) + (0,) * nd, **kw)


def _mixer(x, layer, gmix, wmain, bg, wconv, ghead, wpool, pscale, wout):
    B, T, D = x.shape
    tm = MIX_TM
    row_spec = pl.BlockSpec((1, tm, D), lambda b, t: (b, t, 0))
    params = (gmix, wmain, bg, wconv, ghead, wpool, pscale, wout)
    return pl.pallas_call(
        _mixer_kernel,
        grid=(B, T // tm),
        in_specs=[row_spec] + [_layer_spec(a, layer) for a in params],
        out_specs=row_spec,
        out_shape=jax.ShapeDtypeStruct(x.shape, x.dtype),
        scratch_shapes=[
            pltpu.VMEM((CONV_TAIL, 2 * QK_WIDTH), F32),
            pltpu.VMEM((POOL_TAIL, POOL_WIDTH), F32),
            pltpu.VMEM((HEADS, DK, 2 * DV), F32),
            pltpu.VMEM((SUBLANES, LANES), F32),
            pltpu.VMEM((LANES, tm), F32),
        ],
        compiler_params=pltpu.CompilerParams(
            dimension_semantics=("arbitrary", "arbitrary"), vmem_limit_bytes=VMEM_LIMIT),
        name="mixer",
    )(x, *params)


def _ffn_kernel(x_ref, gffn_ref, wup_ref, wconv_ref, bconv_ref, wdown_ref, gfinal_ref, out_ref,
                up_tail, act_s, hb_s, *, final_norm):
    tm = FFN_TM
    t = pl.program_id(1)

    @pl.when(t == 0)
    def _():
        up_tail[...] = jnp.zeros_like(up_tail)

    hb_s[...] = _rms(x_ref[0], gffn_ref[...]).astype(BF16)

    def conv(cols, scale=None):
        up = _dot(hb_s[...], wup_ref[:, cols])
        ext = jnp.concatenate([up_tail[:, cols], up], axis=0)
        up_tail[:, cols] = up[tm - CONV_TAIL:, :]
        wc, bias = wconv_ref[:, cols], bconv_ref[:, cols]
        if scale is not None:
            wc, bias = wc * scale, bias * scale
        acc = up * wc[FFN_CONV - 1:FFN_CONV, :] + bias
        for j in range(1, FFN_CONV):
            acc = acc + _shift_rows(ext, j, CONV_TAIL) * wc[FFN_CONV - 1 - j:FFN_CONV - j, :]
        return acc

    for f in range(D_FF // FFN_SUB):
        half_gate = conv(slice(f * FFN_SUB, (f + 1) * FFN_SUB), 0.5)
        val = conv(slice(D_FF + f * FFN_SUB, D_FF + (f + 1) * FFN_SUB))
        act = half_gate * (1.0 + jnp.tanh(half_gate)) * val
        act_s[:, f * FFN_SUB:(f + 1) * FFN_SUB] = act.astype(BF16)

    y = x_ref[0] + _dot(act_s[...], wdown_ref[...])
    if final_norm:
        y = _rms(y, gfinal_ref[...])
    out_ref[0] = y


def _ffn(x, layer, gffn, wup, wconv, bconv, wdown, gfinal, final_norm):
    B, T, D = x.shape
    tm = FFN_TM
    row_spec = pl.BlockSpec((1, tm, D), lambda b, t: (b, t, 0))
    return pl.pallas_call(
        functools.partial(_ffn_kernel, final_norm=final_norm),
        grid=(B, T // tm),
        in_specs=[row_spec, _layer_spec(gffn, layer), _layer_spec(wup, layer, 1), _layer_spec(wconv, layer),
                  _layer_spec(bconv, layer), _layer_spec(wdown, layer, 1), _layer_spec(gfinal, 0)],
        out_specs=row_spec,
        out_shape=jax.ShapeDtypeStruct(x.shape, x.dtype),
        scratch_shapes=[
            pltpu.VMEM((CONV_TAIL, 2 * D_FF), F32),
            pltpu.VMEM((tm, D_FF), BF16),
            pltpu.VMEM((tm, D), BF16),
        ],
        compiler_params=pltpu.CompilerParams(
            dimension_semantics=("arbitrary", "arbitrary"), vmem_limit_bytes=VMEM_LIMIT),
        name="ffn",
    )(x, gffn, wup, wconv, bconv, wdown, gfinal)


def kernel(x, mix_norm, w_in, b_gates, w_qk_conv, head_norm, w_pool, pool_scale, w_out, ffn_norm, w_up,
           w_ffn_conv, b_ffn_conv, w_down, final_norm):
    depth = w_in.shape[0]
    gate_lo = 2 * QK_WIDTH + 2 * V_WIDTH
    rows = lambda a: a[:, None, :]
    wmain = jnp.concatenate(
        [w_in[:, :, :gate_lo], w_in[:, :, gate_lo + GATES:], w_in[:, :, gate_lo:gate_lo + GATES],
         jnp.zeros((depth, D_MODEL, LANES - GATES), F32)], axis=2).astype(BF16)
    bg = rows(jnp.pad(b_gates, ((0, 0), (0, LANES - GATES))))
    wp = w_pool.astype(BF16).reshape(depth, len(POOL_WINDOWS) // 2, 2, POOL_GROUP, POOL_GROUP)
    zero = jnp.zeros_like(wp[:, :, 0])
    wpool = jnp.concatenate([jnp.concatenate([wp[:, :, 0], zero], axis=-1),
                             jnp.concatenate([zero, wp[:, :, 1]], axis=-1)], axis=-2)
    wout, wup, wdown = w_out.astype(BF16), w_up.astype(BF16), w_down.astype(BF16)
    gmix, ghead, pscale, gffn, bconv = map(rows, (mix_norm, head_norm, pool_scale, ffn_norm, b_ffn_conv))
    gfinal = final_norm.reshape(1, 1, -1)
    for l in range(depth):
        x = _mixer(x, l, gmix, wmain, bg, w_qk_conv, ghead, wpool, pscale, wout)
        x = _ffn(x, l, gffn, wup, w_ffn_conv, bconv, wdown, gfinal, final_norm=(l == depth - 1))
    return x
```

```python
import functools

import jax
import jax.numpy as jnp
from jax import lax
from jax.experimental import pallas as pl
from jax.experimental.pallas import tpu as pltpu

D_MODEL = 1024
HEADS = 4
DV = 128
DK = 64
QK_WIDTH = HEADS * DK
V_WIDTH = HEADS * DV
QK_CONV = 4
POOL_WINDOWS = (2, 4, 8, 16)
POOL_GROUP = 128
POOL_WIDTH = len(POOL_WINDOWS) * POOL_GROUP
GATES = 2 * HEADS
MAIN_WIDTH = 2 * QK_WIDTH + 2 * V_WIDTH + POOL_WIDTH
D_FF = 2816
FFN_CONV = 3
EPS = 1e-6
LOG2E = 1.4426950408889634

LANES = 128
SUBLANES = 8
MXU_COLS = 256

MIX_TM = 1024
MIX_GROUP = 256
MIX_CHUNK = 128
FFN_TM = 1024
FFN_SUB = MXU_COLS
PERM_GROUP = SUBLANES * SUBLANES
POOL_TAIL = 16
CONV_TAIL = SUBLANES
VMEM_LIMIT = 56 * 1024 * 1024

F32 = jnp.float32
BF16 = jnp.bfloat16


def _dot(a, b):
    return jnp.dot(a, b, preferred_element_type=F32)


def _rms(x, g):
    return x * lax.rsqrt(jnp.mean(x * x, axis=-1, keepdims=True) + EPS) * g


def _silu(x):
    return x * jax.nn.sigmoid(x)


def _shift_rows(ext, j, tail):
    return pltpu.roll(ext, j, axis=0)[tail:, :]


def _mixer_kernel(x_ref, gmix_ref, wmain_ref, bg_ref, wconv_ref, ghead_ref, wpool_ref,
                  pscale_ref, wout_ref, out_ref, qk_tail, u_tail, c_state, m_state, zt_s, tri_s):
    tm, grp, chunk = MIX_TM, MIX_GROUP, MIX_CHUNK
    ngrp, cpg = tm // grp, grp // chunk
    t = pl.program_id(1)

    @pl.when(t == 0)
    def _():
        qk_tail[...] = jnp.zeros_like(qk_tail)
        u_tail[...] = jnp.zeros_like(u_tail)
        c_state[...] = jnp.zeros_like(c_state)
        m_state[...] = jnp.zeros_like(m_state)
        tri_s[...] = (lax.broadcasted_iota(jnp.int32, (chunk, chunk), 0)
                      >= lax.broadcasted_iota(jnp.int32, (chunk, chunk), 1)).astype(BF16)

    lane = lax.broadcasted_iota(jnp.int32, (chunk, LANES), 1)
    is_f = (lane >= HEADS) & (lane < GATES)
    causal = (lax.broadcasted_iota(jnp.int32, (chunk, chunk), 0)
              >= lax.broadcasted_iota(jnp.int32, (chunk, chunk), 1))
    wc = wconv_ref[...]
    pscale = pscale_ref[...]
    gmix = gmix_ref[...]
    ghead_half = ghead_ref[...] * 0.5
    ones_blk = jnp.ones((chunk, DV), BF16)
    k_pad = jnp.zeros((chunk, MXU_COLS - chunk - DK), BF16)
    rhs_pad = jnp.zeros((MXU_COLS - chunk - DK, 2 * DV), BF16)
    rows_of = lambda c: slice(c * chunk, (c + 1) * chunk)
    lane_in_chunk = lax.broadcasted_iota(jnp.int32, (SUBLANES, grp), 1) % chunk

    def norm(r):
        return _rms(x_ref[0, r * grp:(r + 1) * grp, :], gmix).astype(BF16)

    def front(r, hb, qk_hist, u_hist):
        rows = slice(r * grp, (r + 1) * grp)
        p = _dot(hb, wmain_ref[...])
        gates = p[:, MAIN_WIDTH:] + bg_ref[...]

        qk_pre = p[:, :2 * QK_WIDTH]
        ext = jnp.concatenate([qk_hist, qk_pre], axis=0)
        acc = qk_pre * wc[QK_CONV - 1:QK_CONV, :]
        for j in range(1, QK_CONV):
            acc = acc + _shift_rows(ext, j, CONV_TAIL) * wc[QK_CONV - 1 - j:QK_CONV - j, :]
        qk = _silu(acc)
        f = dict(q=qk[:, :QK_WIDTH].astype(BF16),
                 kt=(qk[:, QK_WIDTH:] * (DK ** -0.5)).T,
                 v=p[:, 2 * QK_WIDTH:2 * QK_WIDTH + V_WIDTH].astype(BF16),
                 o_half=p[:, 2 * QK_WIDTH + V_WIDTH:2 * QK_WIDTH + 2 * V_WIDTH])

        z_parts = []
        for c in range(cpg):
            g_c = gates[rows_of(c), :]
            log_f = jnp.minimum(g_c, 0.0) - jnp.log(1.0 + jnp.exp(-jnp.abs(g_c)))
            gl = jnp.where(is_f, log_f, g_c) * LOG2E
            lf = jnp.where(is_f, gl, 0.0)
            hi = lf.astype(BF16)
            r1 = lf - hi.astype(F32)
            mid = r1.astype(BF16)
            lo = (r1 - mid.astype(F32)).astype(BF16)
            b3 = _dot(tri_s[...], jnp.concatenate([hi, mid, lo], axis=1))
            b_c = b3[:, :LANES] + b3[:, LANES:2 * LANES] + b3[:, 2 * LANES:]
            z_parts.append(jnp.where(is_f, b_c, gl))
        zt = jnp.concatenate(z_parts, axis=0).T
        g8 = zt[:SUBLANES, :] - pltpu.roll(zt[:SUBLANES, :], HEADS, axis=0)
        cg8, sh = g8, 1
        while sh < chunk:
            cg8 = jnp.maximum(cg8, jnp.where(lane_in_chunk >= sh, pltpu.roll(cg8, sh, axis=1), -jnp.inf))
            sh *= 2
        zt_s[:, rows] = zt
        zt_s[SUBLANES:2 * SUBLANES, rows] = cg8
        f["zc"] = zt_s[:, rows].T
        f["g_r"] = {(c, hd): g8[hd:hd + 1, rows_of(c)] for c in range(cpg) for hd in range(HEADS)}

        u = p[:, 2 * QK_WIDTH + 2 * V_WIDTH:MAIN_WIDTH]
        sums = jnp.concatenate([u_hist, u], axis=0)
        pos = (t * tm + r * grp + 1 + lax.broadcasted_iota(jnp.int32, (grp, 1), 0)).astype(F32)
        ds = []
        for g, w in enumerate(POOL_WINDOWS):
            sums = sums + pltpu.roll(sums, w // 2, axis=0)
            mean = sums[POOL_TAIL:, :POOL_GROUP] * (1.0 / jnp.minimum(pos, float(w)))
            ds.append((mean - u[:, g * POOL_GROUP:(g + 1) * POOL_GROUP]).astype(BF16))
            sums = sums[:, POOL_GROUP:]
        ys = []
        for g2 in range(len(POOL_WINDOWS) // 2):
            cols = slice(2 * g2 * POOL_GROUP, 2 * (g2 + 1) * POOL_GROUP)
            y = _dot(jnp.concatenate(ds[2 * g2:2 * g2 + 2], axis=1), wpool_ref[g2]) * pscale[:, cols]
            ys.append(y.astype(BF16))
        f["pool_out"] = _dot(jnp.concatenate(ys, axis=1), wout_ref[V_WIDTH:, :])
        return f, qk_pre[grp - CONV_TAIL:, :], u[grp - POOL_TAIL:, :]

    def mlstm(r, f, states):
        pairs = [(c, hd) for c in range(cpg) for hd in range(HEADS)]

        st = {}
        for c, hd in pairs:
            g_r = f["g_r"][c, hd]
            zc = f["zc"][rows_of(c), :]
            bc = jnp.broadcast_to(zc[:, HEADS + hd:HEADS + hd + 1], (chunk, LANES))
            cgc = jnp.broadcast_to(zc[:, 2 * HEADS + hd:2 * HEADS + hd + 1], (chunk, LANES))
            log_d = jnp.where(causal, g_r - cgc, -jnp.inf)
            g_max = jnp.broadcast_to(jnp.max(g_r, axis=1, keepdims=True), (1, LANES))
            st[c, hd] = dict(g_r=g_r, bc=bc, cgc=cgc, log_d=log_d, g_max=g_max)

        for c, hd in pairs:
            d = st[c, hd]
            q_c = f["q"][rows_of(c), hd * DK:(hd + 1) * DK]
            kt_c = f["kt"][hd * DK:(hd + 1) * DK, rows_of(c)]
            d["s"] = _dot(q_c, kt_c.astype(BF16)) * jnp.exp2(d["log_d"])
            d["ktw"] = (kt_c * jnp.exp2(d["g_r"] - d["g_max"])).astype(BF16)
            d["v_aug"] = jnp.concatenate([f["v"][rows_of(c), hd * DV:(hd + 1) * DV], ones_blk], axis=1)
            d["q_c"] = q_c

        for c, hd in pairs:
            d = st[c, hd]
            d["c_loc"] = _dot(d["ktw"], d["v_aug"])

        new_states = []
        for hd in range(HEADS):
            c_aug, m_prev = states[hd]
            for c in range(cpg):
                d = st[c, hd]
                d["c_prev"], d["m_prev"] = c_aug.astype(BF16), m_prev
                mx = jnp.maximum(m_prev, d["g_max"])
                s_prev, s_loc = jnp.exp2(m_prev - mx), jnp.exp2(d["g_max"] - mx)
                c_aug = (jnp.concatenate([s_prev, s_prev], axis=1) * c_aug
                         + jnp.concatenate([s_loc, s_loc], axis=1) * d["c_loc"])
                m_prev = d["bc"][chunk - 1:chunk, :] + mx
            new_states.append((c_aug, m_prev))

        for c, hd in pairs:
            d = st[c, hd]
            dm = d["cgc"] - d["m_prev"]
            m = d["bc"] + jnp.maximum(d["cgc"], d["m_prev"])
            s_w = (jnp.exp2(jnp.minimum(dm, 0.0)) * d["s"]).astype(BF16)
            q_w = (jnp.exp2(jnp.minimum(-dm, 0.0))[:, :DK] * d["q_c"].astype(F32)).astype(BF16)
            lhs = jnp.concatenate([s_w, q_w, k_pad], axis=1)
            rhs = jnp.concatenate([d["v_aug"], d["c_prev"], rhs_pad], axis=0)
            d["tot"], d["m"] = _dot(lhs, rhs), m

        for c in range(cpg):
            ys = []
            for hd in range(HEADS):
                d = st[c, hd]
                den = d["tot"][:, DV:]
                hh = d["tot"][:, :DV] * (1.0 / jnp.maximum(jnp.abs(den), jnp.exp2(-d["m"])))
                cols = slice(hd * DV, (hd + 1) * DV)
                y = hh * lax.rsqrt(jnp.mean(hh * hh, axis=-1, keepdims=True) + EPS) * ghead_half[:, cols]
                ys.append((y * (1.0 + jnp.tanh(f["o_half"][rows_of(c), cols]))).astype(BF16))
            rows = slice(r * grp + c * chunk, r * grp + (c + 1) * chunk)
            mixed = _dot(jnp.concatenate(ys, axis=1), wout_ref[:V_WIDTH, :]) + f["pool_out"][rows_of(c), :]
            out_ref[0, rows, :] = x_ref[0, rows, :] + mixed
        return new_states

    states = [(c_state[hd], m_state[hd:hd + 1, :]) for hd in range(HEADS)]
    hbs = {0: norm(0)}
    if ngrp > 1:
        hbs[1] = norm(1)
    nxt = front(0, hbs[0], qk_tail[...], u_tail[...])
    for r in range(ngrp):
        f, qk_hist, u_hist = nxt
        if r + 2 < ngrp:
            hbs[r + 2] = norm(r + 2)
        if r + 1 < ngrp:
            nxt = front(r + 1, hbs[r + 1], qk_hist, u_hist)
        states = mlstm(r, f, states)
    qk_tail[...] = nxt[1]
    u_tail[...] = nxt[2]
    for hd in range(HEADS):
        c_state[hd] = states[hd][0]
        m_state[hd:hd + 1, :] = states[hd][1]


def _layer_spec(stacked, layer, buffers=None):
    nd = stacked.ndim - 1
    kw = {} if buffers is None else dict(pipeline_mode=pl.Buffered(buffers))
    return pl.BlockSpec((None,) + stacked.shape[1:], lambda b, t: (layer,) + (0,) * nd, **kw)


def _mixer(x, layer, gmix, wmain, bg, wconv, ghead, wpool, pscale, wout):
    B, T, D = x.shape
    tm = MIX_TM
    row_spec = pl.BlockSpec((1, tm, D), lambda b, t: (b, t, 0))
    params = (gmix, wmain, bg, wconv, ghead, wpool, pscale, wout)
    return pl.pallas_call(
        _mixer_kernel,
        grid=(B, T // tm),
        in_specs=[row_spec] + [_layer_spec(a, layer) for a in params],
        out_specs=row_spec,
        out_shape=jax.ShapeDtypeStruct(x.shape, x.dtype),
        scratch_shapes=[
            pltpu.VMEM((CONV_TAIL, 2 * QK_WIDTH), F32),
            pltpu.VMEM((POOL_TAIL, POOL_WIDTH), F32),
            pltpu.VMEM((HEADS, DK, 2 * DV), F32),
            pltpu.VMEM((SUBLANES, LANES), F32),
            pltpu.VMEM((LANES, tm), F32),
            pltpu.VMEM((MIX_CHUNK, MIX_CHUNK), BF16),
        ],
        compiler_params=pltpu.CompilerParams(
            dimension_semantics=("arbitrary", "arbitrary"), vmem_limit_bytes=VMEM_LIMIT),
        name="mixer",
    )(x, *params)


def _ffn_kernel(x_ref, gffn_ref, wup_ref, wconv_ref, bconv_ref, wdown_ref, gfinal_ref, out_ref,
                up_tail, act_s, hb_s, perm_s, *, final_norm):
    tm, ngrp = FFN_TM, FFN_TM // PERM_GROUP
    t = pl.program_id(1)

    @pl.when(t == 0)
    def _():
        up_tail[...] = jnp.zeros_like(up_tail)

    def permuted(i):
        g, s = divmod(i, SUBLANES)
        return pl.ds(g * PERM_GROUP + s, SUBLANES, stride=SUBLANES)

    ntile = D_MODEL // LANES
    lanes_of = lambda c: slice(c * LANES, (c + 1) * LANES)
    h = _rms(x_ref[0], gffn_ref[...])
    for i in range(tm // SUBLANES):
        for c in range(ntile):
            perm_s[c, permuted(i), :] = h[i * SUBLANES:(i + 1) * SUBLANES, lanes_of(c)]
    hb_s[...] = jnp.concatenate([perm_s[c] for c in range(ntile)], axis=1).astype(BF16)

    last_sublane = lax.broadcasted_iota(jnp.int32, (ngrp, SUBLANES, FFN_SUB), 1) == SUBLANES - 1

    def conv(cols, scale=None):
        up = _dot(hb_s[...], wup_ref[:, cols])
        up4 = up.reshape(ngrp, SUBLANES, SUBLANES, FFN_SUB)
        tail = up_tail[:, cols].reshape(FFN_CONV - 1, SUBLANES, FFN_SUB)
        up_tail[:, cols] = up[tm - (FFN_CONV - 1) * SUBLANES:, :]
        wc, bias = wconv_ref[:, cols], bconv_ref[:, cols]
        if scale is not None:
            wc, bias = wc * scale, bias * scale
        wrapped = []
        for k in range(FFN_CONV - 1):
            v = SUBLANES - (FFN_CONV - 1) + k
            here = up4[:, v]
            before = jnp.concatenate([tail[k:k + 1], here[:-1]], axis=0)
            mixed = jnp.where(last_sublane, before, here)
            wrapped.append(jnp.stack([pltpu.roll(mixed[g], 1, axis=0) for g in range(ngrp)])[:, None])
        acc = up * wc[FFN_CONV - 1:FFN_CONV, :] + bias
        for j in range(1, FFN_CONV):
            delayed = jnp.concatenate(wrapped[FFN_CONV - 1 - j:] + [up4[:, :SUBLANES - j]], axis=1)
            acc = acc + delayed.reshape(tm, FFN_SUB) * wc[FFN_CONV - 1 - j:FFN_CONV - j, :]
        return acc

    for f in range(D_FF // FFN_SUB):
        half_gate = conv(slice(f * FFN_SUB, (f + 1) * FFN_SUB), 0.5)
        val = conv(slice(D_FF + f * FFN_SUB, D_FF + (f + 1) * FFN_SUB))
        act = half_gate * (1.0 + jnp.tanh(half_gate)) * val
        act_s[:, f * FFN_SUB:(f + 1) * FFN_SUB] = act.astype(BF16)

    down = _dot(act_s[...], wdown_ref[...])
    for c in range(ntile):
        perm_s[c] = down[:, lanes_of(c)]
    for i in range(tm // SUBLANES):
        rows = slice(i * SUBLANES, (i + 1) * SUBLANES)
        y = x_ref[0, rows, :] + jnp.concatenate([perm_s[c, permuted(i), :] for c in range(ntile)], axis=1)
        if final_norm:
            y = _rms(y, gfinal_ref[...])
        out_ref[0, rows, :] = y


def _ffn(x, layer, gffn, wup, wconv, bconv, wdown, gfinal, final_norm):
    B, T, D = x.shape
    tm = FFN_TM
    row_spec = pl.BlockSpec((1, tm, D), lambda b, t: (b, t, 0))
    return pl.pallas_call(
        functools.partial(_ffn_kernel, final_norm=final_norm),
        grid=(B, T // tm),
        in_specs=[row_spec, _layer_spec(gffn, layer), _layer_spec(wup, layer, 1), _layer_spec(wconv, layer),
                  _layer_spec(bconv, layer), _layer_spec(wdown, layer, 1), _layer_spec(gfinal, 0)],
        out_specs=row_spec,
        out_shape=jax.ShapeDtypeStruct(x.shape, x.dtype),
        scratch_shapes=[
            pltpu.VMEM(((FFN_CONV - 1) * SUBLANES, 2 * D_FF), F32),
            pltpu.VMEM((tm, D_FF), BF16),
            pltpu.VMEM((tm, D), BF16),
            pltpu.VMEM((D // LANES, tm, LANES), F32),
        ],
        compiler_params=pltpu.CompilerParams(
            dimension_semantics=("arbitrary", "arbitrary"), vmem_limit_bytes=VMEM_LIMIT),
        name="ffn",
    )(x, gffn, wup, wconv, bconv, wdown, gfinal)


def kernel(x, mix_norm, w_in, b_gates, w_qk_conv, head_norm, w_pool, pool_scale, w_out, ffn_norm, w_up,
           w_ffn_conv, b_ffn_conv, w_down, final_norm):
    depth = w_in.shape[0]
    gate_lo = 2 * QK_WIDTH + 2 * V_WIDTH
    rows = lambda a: a[:, None, :]
    o_lo = 2 * QK_WIDTH + V_WIDTH
    wmain = jnp.concatenate(
        [w_in[:, :, :o_lo], w_in[:, :, o_lo:gate_lo] * 0.5, w_in[:, :, gate_lo + GATES:],
         w_in[:, :, gate_lo:gate_lo + GATES], jnp.zeros((depth, D_MODEL, LANES - GATES), F32)],
        axis=2).astype(BF16)
    bg = rows(jnp.pad(b_gates, ((0, 0), (0, LANES - GATES))))
    wp = w_pool.astype(BF16).reshape(depth, len(POOL_WINDOWS) // 2, 2, POOL_GROUP, POOL_GROUP)
    zero = jnp.zeros_like(wp[:, :, 0])
    wpool = jnp.concatenate([jnp.concatenate([wp[:, :, 0], zero], axis=-1),
                             jnp.concatenate([zero, wp[:, :, 1]], axis=-1)], axis=-2)
    wout, wup, wdown = w_out.astype(BF16), w_up.astype(BF16), w_down.astype(BF16)
    gmix, ghead, pscale, gffn, bconv = map(rows, (mix_norm, head_norm, pool_scale, ffn_norm, b_ffn_conv))
    gfinal = final_norm.reshape(1, 1, -1)
    for l in range(depth):
        x = _mixer(x, l, gmix, wmain, bg, w_qk_conv, ghead, wpool, pscale, wout)
        x = _ffn(x, l, gffn, wup, w_ffn_conv, bconv, wdown, gfinal, final_norm=(l == depth - 1))
    return x
```

```python
import functools

import jax
import jax.numpy as jnp
from jax import lax
from jax.experimental import pallas as pl
from jax.experimental.pallas import tpu as pltpu

D_MODEL = 1024
HEADS = 4
DV = 128
DK = 64
QK_WIDTH = HEADS * DK
V_WIDTH = HEADS * DV
QK_CONV = 4
POOL_WINDOWS = (2, 4, 8, 16)
POOL_GROUP = 128
POOL_WIDTH = len(POOL_WINDOWS) * POOL_GROUP
GATES = 2 * HEADS
MAIN_WIDTH = 2 * QK_WIDTH + 2 * V_WIDTH + POOL_WIDTH
D_FF = 2816
FFN_CONV = 3
EPS = 1e-6

LANES = 128
SUBLANES = 8
MXU_COLS = 256

MIX_TM = 1024
MIX_GROUP = 256
MIX_CHUNK = 128
FFN_TM = 1024
FFN_SUB = MXU_COLS
POOL_TAIL = 16
CONV_TAIL = SUBLANES
VMEM_LIMIT = 56 * 1024 * 1024

F32 = jnp.float32
BF16 = jnp.bfloat16


def _dot(a, b):
    return jnp.dot(a, b, preferred_element_type=F32)


def _rms(x, g):
    return x * lax.rsqrt(jnp.mean(x * x, axis=-1, keepdims=True) + EPS) * g


def _silu(x):
    return x * jax.nn.sigmoid(x)


def _shift_rows(ext, j, tail):
    return pltpu.roll(ext, j, axis=0)[tail:, :]


def _mixer_kernel(x_ref, gmix_ref, wmain_ref, bg_ref, wconv_ref, ghead_ref, wpool_ref,
                  pscale_ref, wout_ref, wup_f32_ref, wdown_f32_ref, out_ref, wup_ref, wdown_ref,
                  qk_tail, u_tail, c_state, m_state, zt_s):
    wup_ref[...] = wup_f32_ref[...].astype(BF16)
    wdown_ref[...] = wdown_f32_ref[...].astype(BF16)

    tm, grp, chunk = MIX_TM, MIX_GROUP, MIX_CHUNK
    ngrp, cpg = tm // grp, grp // chunk
    t = pl.program_id(1)

    @pl.when(t == 0)
    def _():
        qk_tail[...] = jnp.zeros_like(qk_tail)
        u_tail[...] = jnp.zeros_like(u_tail)
        c_state[...] = jnp.zeros_like(c_state)
        m_state[...] = jnp.zeros_like(m_state)

    lane = lax.broadcasted_iota(jnp.int32, (chunk, LANES), 1)
    is_f = (lane >= HEADS) & (lane < GATES)
    causal = (lax.broadcasted_iota(jnp.int32, (chunk, chunk), 0)
              >= lax.broadcasted_iota(jnp.int32, (chunk, chunk), 1))
    tri = causal.astype(BF16)
    wc = wconv_ref[...]
    pscale = pscale_ref[...]
    gmix = gmix_ref[...]
    ghead = ghead_ref[...]
    ones_blk = jnp.ones((chunk, DV), BF16)
    k_pad = jnp.zeros((chunk, MXU_COLS - chunk - DK), BF16)
    rhs_pad = jnp.zeros((MXU_COLS - chunk - DK, 2 * DV), BF16)
    rows_of = lambda c: slice(c * chunk, (c + 1) * chunk)

    def front(r, qk_hist, u_hist):
        rows = slice(r * grp, (r + 1) * grp)
        hb = _rms(x_ref[0, rows, :], gmix).astype(BF16)
        p = _dot(hb, wmain_ref[...])
        gates = p[:, MAIN_WIDTH:] + bg_ref[...]

        qk_pre = p[:, :2 * QK_WIDTH]
        ext = jnp.concatenate([qk_hist, qk_pre], axis=0)
        acc = qk_pre * wc[QK_CONV - 1:QK_CONV, :]
        for j in range(1, QK_CONV):
            acc = acc + _shift_rows(ext, j, CONV_TAIL) * wc[QK_CONV - 1 - j:QK_CONV - j, :]
        qk = _silu(acc)
        f = dict(q=qk[:, :QK_WIDTH].astype(BF16),
                 kt=(qk[:, QK_WIDTH:] * (DK ** -0.5)).T,
                 v=p[:, 2 * QK_WIDTH:2 * QK_WIDTH + V_WIDTH].astype(BF16),
                 o=p[:, 2 * QK_WIDTH + V_WIDTH:2 * QK_WIDTH + 2 * V_WIDTH])

        z_parts = []
        for c in range(cpg):
            g_c = gates[rows_of(c), :]
            lf = jnp.where(is_f, jnp.minimum(g_c, 0.0) - jnp.log(1.0 + jnp.exp(-jnp.abs(g_c))), 0.0)
            hi = lf.astype(BF16)
            r1 = lf - hi.astype(F32)
            mid = r1.astype(BF16)
            lo = (r1 - mid.astype(F32)).astype(BF16)
            b3 = _dot(tri, jnp.concatenate([hi, mid, lo], axis=1))
            b_c = b3[:, :LANES] + b3[:, LANES:2 * LANES] + b3[:, 2 * LANES:]
            z_parts.append(jnp.where(is_f, b_c, g_c))
        f["z"] = jnp.concatenate(z_parts, axis=0)
        zt_s[:, rows] = f["z"].T
        f["g_r"] = {}
        for c in range(cpg):
            lanes = slice(r * grp + c * chunk, r * grp + (c + 1) * chunk)
            for hd in range(HEADS):
                f["g_r"][c, hd] = zt_s[hd:hd + 1, lanes] - zt_s[HEADS + hd:HEADS + hd + 1, lanes]

        u = p[:, 2 * QK_WIDTH + 2 * V_WIDTH:MAIN_WIDTH]
        sums = jnp.concatenate([u_hist, u], axis=0)
        pos = (t * tm + r * grp + 1 + lax.broadcasted_iota(jnp.int32, (grp, 1), 0)).astype(F32)
        ds = []
        for g, w in enumerate(POOL_WINDOWS):
            sums = sums + pltpu.roll(sums, w // 2, axis=0)
            mean = sums[POOL_TAIL:, :POOL_GROUP] * (1.0 / jnp.minimum(pos, float(w)))
            ds.append((mean - u[:, g * POOL_GROUP:(g + 1) * POOL_GROUP]).astype(BF16))
            sums = sums[:, POOL_GROUP:]
        ys = []
        for g2 in range(len(POOL_WINDOWS) // 2):
            cols = slice(2 * g2 * POOL_GROUP, 2 * (g2 + 1) * POOL_GROUP)
            y = _dot(jnp.concatenate(ds[2 * g2:2 * g2 + 2], axis=1), wpool_ref[g2]) * pscale[:, cols]
            ys.append(y.astype(BF16))
        f["pool_out"] = _dot(jnp.concatenate(ys, axis=1), wout_ref[V_WIDTH:, :])
        return f, qk_pre[grp - CONV_TAIL:, :], u[grp - POOL_TAIL:, :]

    def mlstm(r, f, states):
        pairs = [(c, hd) for c in range(cpg) for hd in range(HEADS)]

        st = {}
        for c, hd in pairs:
            g_r = f["g_r"][c, hd]
            bc = jnp.broadcast_to(f["z"][rows_of(c), HEADS + hd:HEADS + hd + 1], (chunk, LANES))
            log_d = jnp.where(causal, bc + g_r, -jnp.inf)
            m_intra = jnp.broadcast_to(jnp.max(log_d, axis=1, keepdims=True), (chunk, LANES))
            g_max = jnp.broadcast_to(jnp.max(g_r, axis=1, keepdims=True), (1, LANES))
            st[c, hd] = dict(g_r=g_r, bc=bc, log_d=log_d, m_intra=m_intra, g_max=g_max)

        for c, hd in pairs:
            d = st[c, hd]
            q_c = f["q"][rows_of(c), hd * DK:(hd + 1) * DK]
            kt_c = f["kt"][hd * DK:(hd + 1) * DK, rows_of(c)]
            dmat = jnp.exp(d["log_d"] - d["m_intra"])
            d["s"] = _dot(q_c, kt_c.astype(BF16)) * dmat
            d["ktw"] = (kt_c * jnp.exp(d["g_r"] - d["g_max"])).astype(BF16)
            d["v_aug"] = jnp.concatenate([f["v"][rows_of(c), hd * DV:(hd + 1) * DV], ones_blk], axis=1)
            d["q_c"] = q_c

        for c, hd in pairs:
            d = st[c, hd]
            d["c_loc"] = _dot(d["ktw"], d["v_aug"])

        new_states = []
        for hd in range(HEADS):
            c_aug, m_prev = states[hd]
            for c in range(cpg):
                d = st[c, hd]
                d["c_prev"], d["m_prev"] = c_aug.astype(BF16), m_prev
                mx = jnp.maximum(m_prev, d["g_max"])
                s_prev, s_loc = jnp.exp(m_prev - mx), jnp.exp(d["g_max"] - mx)
                c_aug = (jnp.concatenate([s_prev, s_prev], axis=1) * c_aug
                         + jnp.concatenate([s_loc, s_loc], axis=1) * d["c_loc"])
                m_prev = d["bc"][chunk - 1:chunk, :] + mx
            new_states.append((c_aug, m_prev))

        for c, hd in pairs:
            d = st[c, hd]
            m_inter = d["bc"] + d["m_prev"]
            m = jnp.maximum(m_inter, d["m_intra"])
            s_w = (jnp.exp(d["m_intra"] - m) * d["s"]).astype(BF16)
            q_w = (jnp.exp(m_inter - m)[:, :DK] * d["q_c"].astype(F32)).astype(BF16)
            lhs = jnp.concatenate([s_w, q_w, k_pad], axis=1)
            rhs = jnp.concatenate([d["v_aug"], d["c_prev"], rhs_pad], axis=0)
            d["tot"], d["m"] = _dot(lhs, rhs), m

        for c in range(cpg):
            ys = []
            for hd in range(HEADS):
                d = st[c, hd]
                den = d["tot"][:, DV:]
                hh = d["tot"][:, :DV] * (1.0 / jnp.maximum(jnp.abs(den), jnp.exp(-d["m"])))
                cols = slice(hd * DV, (hd + 1) * DV)
                y = hh * lax.rsqrt(jnp.mean(hh * hh, axis=-1, keepdims=True) + EPS) * ghead[:, cols]
                ys.append((y * jax.nn.sigmoid(f["o"][rows_of(c), cols])).astype(BF16))
            rows = slice(r * grp + c * chunk, r * grp + (c + 1) * chunk)
            mixed = _dot(jnp.concatenate(ys, axis=1), wout_ref[:V_WIDTH, :]) + f["pool_out"][rows_of(c), :]
            out_ref[0, rows, :] = x_ref[0, rows, :] + mixed
        return new_states

    states = [(c_state[hd], m_state[hd:hd + 1, :]) for hd in range(HEADS)]
    nxt = front(0, qk_tail[...], u_tail[...])
    for r in range(ngrp):
        f, qk_hist, u_hist = nxt
        if r + 1 < ngrp:
            nxt = front(r + 1, qk_hist, u_hist)
        states = mlstm(r, f, states)
    qk_tail[...] = nxt[1]
    u_tail[...] = nxt[2]
    for hd in range(HEADS):
        c_state[hd] = states[hd][0]
        m_state[hd:hd + 1, :] = states[hd][1]


def _layer_spec(stacked, layer, buffers=None):
    nd = stacked.ndim - 1
    kw = {} if buffers is None else dict(pipeline_mode=pl.Buffered(buffers))
    return pl.BlockSpec((None,) + stacked.shape[1:], lambda b, t: (layer,) + (0,) * nd, **kw)


def _whole_spec(a):
    return pl.BlockSpec(a.shape, lambda b, t: (0,) * a.ndim, pipeline_mode=pl.Buffered(1))


def _mixer(x, layer, gmix, wmain, bg, wconv, ghead, wpool, pscale, wout, w_up, w_down):
    B, T, D = x.shape
    tm = MIX_TM
    steps = B * (T // tm)
    row_spec = pl.BlockSpec((1, tm, D), lambda b, t: (b, t, 0))
    params = (gmix, wmain, bg, wconv, ghead, wpool, pscale, wout)

    def slab(w):
        return w.shape[1] // steps

    def cast_spec(w):
        return pl.BlockSpec((None, slab(w), w.shape[2]), lambda b, t: (layer, b * (T // tm) + t, 0))

    def cast_out_spec(w):
        return pl.BlockSpec((slab(w), w.shape[2]), lambda b, t: (b * (T // tm) + t, 0))

    return pl.pallas_call(
        _mixer_kernel,
        grid=(B, T // tm),
        in_specs=[row_spec] + [_layer_spec(a, layer) for a in params] + [cast_spec(w_up), cast_spec(w_down)],
        out_specs=[row_spec, cast_out_spec(w_up), cast_out_spec(w_down)],
        out_shape=[jax.ShapeDtypeStruct(x.shape, x.dtype), jax.ShapeDtypeStruct(w_up.shape[1:], BF16),
                   jax.ShapeDtypeStruct(w_down.shape[1:], BF16)],
        scratch_shapes=[
            pltpu.VMEM((CONV_TAIL, 2 * QK_WIDTH), F32),
            pltpu.VMEM((POOL_TAIL, POOL_WIDTH), F32),
            pltpu.VMEM((HEADS, DK, 2 * DV), F32),
            pltpu.VMEM((SUBLANES, LANES), F32),
            pltpu.VMEM((LANES, tm), F32),
        ],
        compiler_params=pltpu.CompilerParams(
            dimension_semantics=("arbitrary", "arbitrary"), vmem_limit_bytes=VMEM_LIMIT),
        name="mixer",
    )(x, *params, w_up, w_down)


def _ffn_kernel(x_ref, gffn_ref, wup_ref, wconv_ref, bconv_ref, wdown_ref, gfinal_ref, out_ref,
                up_tail, act_s, hb_s, *, final_norm):
    tm = FFN_TM
    t = pl.program_id(1)

    @pl.when(t == 0)
    def _():
        up_tail[...] = jnp.zeros_like(up_tail)

    hb_s[...] = _rms(x_ref[0], gffn_ref[...]).astype(BF16)

    def conv(cols, scale=None):
        up = _dot(hb_s[...], wup_ref[:, cols])
        ext = jnp.concatenate([up_tail[:, cols], up], axis=0)
        up_tail[:, cols] = up[tm - CONV_TAIL:, :]
        wc, bias = wconv_ref[:, cols], bconv_ref[:, cols]
        if scale is not None:
            wc, bias = wc * scale, bias * scale
        acc = up * wc[FFN_CONV - 1:FFN_CONV, :] + bias
        for j in range(1, FFN_CONV):
            acc = acc + _shift_rows(ext, j, CONV_TAIL) * wc[FFN_CONV - 1 - j:FFN_CONV - j, :]
        return acc

    for f in range(D_FF // FFN_SUB):
        half_gate = conv(slice(f * FFN_SUB, (f + 1) * FFN_SUB), 0.5)
        val = conv(slice(D_FF + f * FFN_SUB, D_FF + (f + 1) * FFN_SUB))
        act = half_gate * (1.0 + jnp.tanh(half_gate)) * val
        act_s[:, f * FFN_SUB:(f + 1) * FFN_SUB] = act.astype(BF16)

    y = x_ref[0] + _dot(act_s[...], wdown_ref[...])
    if final_norm:
        y = _rms(y, gfinal_ref[...])
    out_ref[0] = y


def _ffn(x, layer, gffn, wup, wconv, bconv, wdown, gfinal, final_norm):
    B, T, D = x.shape
    tm = FFN_TM
    row_spec = pl.BlockSpec((1, tm, D), lambda b, t: (b, t, 0))
    return pl.pallas_call(
        functools.partial(_ffn_kernel, final_norm=final_norm),
        grid=(B, T // tm),
        in_specs=[row_spec, _layer_spec(gffn, layer), _whole_spec(wup), _layer_spec(wconv, layer),
                  _layer_spec(bconv, layer), _whole_spec(wdown), _layer_spec(gfinal, 0)],
        out_specs=row_spec,
        out_shape=jax.ShapeDtypeStruct(x.shape, x.dtype),
        scratch_shapes=[
            pltpu.VMEM((CONV_TAIL, 2 * D_FF), F32),
            pltpu.VMEM((tm, D_FF), BF16),
            pltpu.VMEM((tm, D), BF16),
        ],
        compiler_params=pltpu.CompilerParams(
            dimension_semantics=("arbitrary", "arbitrary"), vmem_limit_bytes=VMEM_LIMIT),
        name="ffn",
    )(x, gffn, wup, wconv, bconv, wdown, gfinal)


def kernel(x, mix_norm, w_in, b_gates, w_qk_conv, head_norm, w_pool, pool_scale, w_out, ffn_norm, w_up,
           w_ffn_conv, b_ffn_conv, w_down, final_norm):
    depth = w_in.shape[0]
    rows = lambda a: a[:, None, :]
    gate_lo = 2 * QK_WIDTH + 2 * V_WIDTH
    wmain = jnp.concatenate(
        [w_in[:, :, :gate_lo], w_in[:, :, gate_lo + GATES:], w_in[:, :, gate_lo:gate_lo + GATES],
         jnp.zeros((depth, D_MODEL, LANES - GATES), F32)], axis=2).astype(BF16)
    bg = rows(jnp.pad(b_gates, ((0, 0), (0, LANES - GATES))))
    wp = w_pool.astype(BF16).reshape(depth, len(POOL_WINDOWS) // 2, 2, POOL_GROUP, POOL_GROUP)
    zero = jnp.zeros_like(wp[:, :, 0])
    wpool = jnp.concatenate([jnp.concatenate([wp[:, :, 0], zero], axis=-1),
                             jnp.concatenate([zero, wp[:, :, 1]], axis=-1)], axis=-2)
    wout = w_out.astype(BF16)
    gmix, ghead, pscale, gffn, bconv = map(rows, (mix_norm, head_norm, pool_scale, ffn_norm, b_ffn_conv))
    gfinal = final_norm.reshape(1, 1, -1)
    for l in range(depth):
        x, wup, wdown = _mixer(x, l, gmix, wmain, bg, w_qk_conv, ghead, wpool, pscale, wout, w_up, w_down)
        x = _ffn(x, l, gffn, wup, w_ffn_conv, bconv, wdown, gfinal, final_norm=(l == depth - 1))
    return x
```

```python
import functools

import jax
import jax.numpy as jnp
from jax import lax
from jax.experimental import pallas as pl
from jax.experimental.pallas import tpu as pltpu

D_MODEL = 1024
HEADS = 4
DV = 128
DK = 64
QK_WIDTH = HEADS * DK
V_WIDTH = HEADS * DV
QK_CONV = 4
POOL_WINDOWS = (2, 4, 8, 16)
POOL_GROUP = 128
POOL_WIDTH = len(POOL_WINDOWS) * POOL_GROUP
GATES = 2 * HEADS
GATE_LO = 2 * QK_WIDTH + 2 * V_WIDTH
MAIN_WIDTH = GATE_LO + POOL_WIDTH
D_FF = 2816
FFN_CONV = 3
EPS = 1e-6

LANES = 128
SUBLANES = 8
MXU_COLS = 256

MIX_TM = 1024
MIX_GROUP = 256
MIX_CHUNK = 128
FFN_TM = 1024
FFN_SUB = MXU_COLS
POOL_TAIL = 16
CONV_TAIL = SUBLANES
VMEM_LIMIT = 56 * 1024 * 1024

F32 = jnp.float32
BF16 = jnp.bfloat16


def _dot(a, b):
    return jnp.dot(a, b, preferred_element_type=F32)


def _rms(x, g):
    return x * lax.rsqrt(jnp.mean(x * x, axis=-1, keepdims=True) + EPS) * g


def _silu(x):
    return x * jax.nn.sigmoid(x)


def _shift_rows(ext, j, tail):
    return pltpu.roll(ext, j, axis=0)[tail:, :]


def _mixer_kernel(x_ref, gmix_ref, win_ref, bg_ref, wconv_ref, ghead_ref, wpool_ref,
                  pscale_ref, wout_f32_ref, wup_f32_ref, wdown_f32_ref, out_ref, wup_ref, wdown_ref,
                  qk_tail, u_tail, c_state, m_state, zt_s, wmain_ref, wout_ref):
    wup_ref[...] = wup_f32_ref[...].astype(BF16)
    wdown_ref[...] = wdown_f32_ref[...].astype(BF16)

    tm, grp, chunk = MIX_TM, MIX_GROUP, MIX_CHUNK
    ngrp, cpg = tm // grp, grp // chunk
    t = pl.program_id(1)

    @pl.when((pl.program_id(0) == 0) & (t == 0))
    def _():
        for rows in (slice(r, r + MIX_GROUP) for r in range(0, D_MODEL, MIX_GROUP)):
            w = win_ref[rows, :]
            gates_w = jnp.concatenate([w[:, GATE_LO:GATE_LO + GATES],
                                       jnp.zeros((MIX_GROUP, LANES - GATES), F32)], axis=1)
            wmain_ref[rows, :GATE_LO] = w[:, :GATE_LO].astype(BF16)
            wmain_ref[rows, GATE_LO:MAIN_WIDTH] = w[:, GATE_LO + GATES:].astype(BF16)
            wmain_ref[rows, MAIN_WIDTH:] = gates_w.astype(BF16)
            wout_ref[rows, :] = wout_f32_ref[rows, :].astype(BF16)

    @pl.when(t == 0)
    def _():
        qk_tail[...] = jnp.zeros_like(qk_tail)
        u_tail[...] = jnp.zeros_like(u_tail)
        c_state[...] = jnp.zeros_like(c_state)
        m_state[...] = jnp.zeros_like(m_state)

    lane = lax.broadcasted_iota(jnp.int32, (chunk, LANES), 1)
    is_f = (lane >= HEADS) & (lane < GATES)
    causal = (lax.broadcasted_iota(jnp.int32, (chunk, chunk), 0)
              >= lax.broadcasted_iota(jnp.int32, (chunk, chunk), 1))
    tri = causal.astype(BF16)
    wc = wconv_ref[...]
    pscale = pscale_ref[...]
    gmix = gmix_ref[...]
    ghead = ghead_ref[...]
    ones_blk = jnp.ones((chunk, DV), BF16)
    k_pad = jnp.zeros((chunk, MXU_COLS - chunk - DK), BF16)
    rhs_pad = jnp.zeros((MXU_COLS - chunk - DK, 2 * DV), BF16)
    rows_of = lambda c: slice(c * chunk, (c + 1) * chunk)

    def front(r, qk_hist, u_hist):
        rows = slice(r * grp, (r + 1) * grp)
        hb = _rms(x_ref[0, rows, :], gmix).astype(BF16)
        p = _dot(hb, wmain_ref[...])
        gates = p[:, MAIN_WIDTH:] + bg_ref[...]

        qk_pre = p[:, :2 * QK_WIDTH]
        ext = jnp.concatenate([qk_hist, qk_pre], axis=0)
        acc = qk_pre * wc[QK_CONV - 1:QK_CONV, :]
        for j in range(1, QK_CONV):
            acc = acc + _shift_rows(ext, j, CONV_TAIL) * wc[QK_CONV - 1 - j:QK_CONV - j, :]
        qk = _silu(acc)
        f = dict(q=qk[:, :QK_WIDTH].astype(BF16),
                 kt=(qk[:, QK_WIDTH:] * (DK ** -0.5)).T,
                 v=p[:, 2 * QK_WIDTH:2 * QK_WIDTH + V_WIDTH].astype(BF16),
                 o=p[:, 2 * QK_WIDTH + V_WIDTH:2 * QK_WIDTH + 2 * V_WIDTH])

        z_parts = []
        for c in range(cpg):
            g_c = gates[rows_of(c), :]
            lf = jnp.where(is_f, jnp.minimum(g_c, 0.0) - jnp.log(1.0 + jnp.exp(-jnp.abs(g_c))), 0.0)
            hi = lf.astype(BF16)
            r1 = lf - hi.astype(F32)
            mid = r1.astype(BF16)
            lo = (r1 - mid.astype(F32)).astype(BF16)
            b3 = _dot(tri, jnp.concatenate([hi, mid, lo], axis=1))
            b_c = b3[:, :LANES] + b3[:, LANES:2 * LANES] + b3[:, 2 * LANES:]
            z_parts.append(jnp.where(is_f, b_c, g_c))
        f["z"] = jnp.concatenate(z_parts, axis=0)
        zt_s[:, rows] = f["z"].T
        f["g_r"] = {}
        for c in range(cpg):
            lanes = slice(r * grp + c * chunk, r * grp + (c + 1) * chunk)
            for hd in range(HEADS):
                f["g_r"][c, hd] = zt_s[hd:hd + 1, lanes] - zt_s[HEADS + hd:HEADS + hd + 1, lanes]

        u = p[:, 2 * QK_WIDTH + 2 * V_WIDTH:MAIN_WIDTH]
        sums = jnp.concatenate([u_hist, u], axis=0)
        pos = (t * tm + r * grp + 1 + lax.broadcasted_iota(jnp.int32, (grp, 1), 0)).astype(F32)
        ds = []
        for g, w in enumerate(POOL_WINDOWS):
            sums = sums + pltpu.roll(sums, w // 2, axis=0)
            mean = sums[POOL_TAIL:, :POOL_GROUP] * (1.0 / jnp.minimum(pos, float(w)))
            ds.append((mean - u[:, g * POOL_GROUP:(g + 1) * POOL_GROUP]).astype(BF16))
            sums = sums[:, POOL_GROUP:]
        ys = []
        for g2 in range(len(POOL_WINDOWS) // 2):
            cols = slice(2 * g2 * POOL_GROUP, 2 * (g2 + 1) * POOL_GROUP)
            y = _dot(jnp.concatenate(ds[2 * g2:2 * g2 + 2], axis=1), wpool_ref[g2]) * pscale[:, cols]
            ys.append(y.astype(BF16))
        f["pool_out"] = _dot(jnp.concatenate(ys, axis=1), wout_ref[V_WIDTH:, :])
        return f, qk_pre[grp - CONV_TAIL:, :], u[grp - POOL_TAIL:, :]

    def mlstm(r, f, states):
        pairs = [(c, hd) for c in range(cpg) for hd in range(HEADS)]

        st = {}
        for c, hd in pairs:
            g_r = f["g_r"][c, hd]
            bc = jnp.broadcast_to(f["z"][rows_of(c), HEADS + hd:HEADS + hd + 1], (chunk, LANES))
            log_d = jnp.where(causal, bc + g_r, -jnp.inf)
            m_intra = jnp.broadcast_to(jnp.max(log_d, axis=1, keepdims=True), (chunk, LANES))
            g_max = jnp.broadcast_to(jnp.max(g_r, axis=1, keepdims=True), (1, LANES))
            st[c, hd] = dict(g_r=g_r, bc=bc, log_d=log_d, m_intra=m_intra, g_max=g_max)

        for c, hd in pairs:
            d = st[c, hd]
            q_c = f["q"][rows_of(c), hd * DK:(hd + 1) * DK]
            kt_c = f["kt"][hd * DK:(hd + 1) * DK, rows_of(c)]
            dmat = jnp.exp(d["log_d"] - d["m_intra"])
            d["s"] = _dot(q_c, kt_c.astype(BF16)) * dmat
            d["ktw"] = (kt_c * jnp.exp(d["g_r"] - d["g_max"])).astype(BF16)
            d["v_aug"] = jnp.concatenate([f["v"][rows_of(c), hd * DV:(hd + 1) * DV], ones_blk], axis=1)
            d["q_c"] = q_c

        for c, hd in pairs:
            d = st[c, hd]
            d["c_loc"] = _dot(d["ktw"], d["v_aug"])

        new_states = []
        for hd in range(HEADS):
            c_aug, m_prev = states[hd]
            for c in range(cpg):
                d = st[c, hd]
                d["c_prev"], d["m_prev"] = c_aug.astype(BF16), m_prev
                mx = jnp.maximum(m_prev, d["g_max"])
                s_prev, s_loc = jnp.exp(m_prev - mx), jnp.exp(d["g_max"] - mx)
                c_aug = (jnp.concatenate([s_prev, s_prev], axis=1) * c_aug
                         + jnp.concatenate([s_loc, s_loc], axis=1) * d["c_loc"])
                m_prev = d["bc"][chunk - 1:chunk, :] + mx
            new_states.append((c_aug, m_prev))

        for c, hd in pairs:
            d = st[c, hd]
            m_inter = d["bc"] + d["m_prev"]
            m = jnp.maximum(m_inter, d["m_intra"])
            s_w = (jnp.exp(d["m_intra"] - m) * d["s"]).astype(BF16)
            q_w = (jnp.exp(m_inter - m)[:, :DK] * d["q_c"].astype(F32)).astype(BF16)
            lhs = jnp.concatenate([s_w, q_w, k_pad], axis=1)
            rhs = jnp.concatenate([d["v_aug"], d["c_prev"], rhs_pad], axis=0)
            d["tot"], d["m"] = _dot(lhs, rhs), m

        for c in range(cpg):
            ys = []
            for hd in range(HEADS):
                d = st[c, hd]
                den = d["tot"][:, DV:]
                hh = d["tot"][:, :DV] * (1.0 / jnp.maximum(jnp.abs(den), jnp.exp(-d["m"])))
                cols = slice(hd * DV, (hd + 1) * DV)
                y = hh * lax.rsqrt(jnp.mean(hh * hh, axis=-1, keepdims=True) + EPS) * ghead[:, cols]
                ys.append((y * jax.nn.sigmoid(f["o"][rows_of(c), cols])).astype(BF16))
            rows = slice(r * grp + c * chunk, r * grp + (c + 1) * chunk)
            mixed = _dot(jnp.concatenate(ys, axis=1), wout_ref[:V_WIDTH, :]) + f["pool_out"][rows_of(c), :]
            out_ref[0, rows, :] = x_ref[0, rows, :] + mixed
        return new_states

    states = [(c_state[hd], m_state[hd:hd + 1, :]) for hd in range(HEADS)]
    nxt = front(0, qk_tail[...], u_tail[...])
    for r in range(ngrp):
        f, qk_hist, u_hist = nxt
        if r + 1 < ngrp:
            nxt = front(r + 1, qk_hist, u_hist)
        states = mlstm(r, f, states)
    qk_tail[...] = nxt[1]
    u_tail[...] = nxt[2]
    for hd in range(HEADS):
        c_state[hd] = states[hd][0]
        m_state[hd:hd + 1, :] = states[hd][1]


def _layer_spec(stacked, layer, buffers=None):
    nd = stacked.ndim - 1
    kw = {} if buffers is None else dict(pipeline_mode=pl.Buffered(buffers))
    return pl.BlockSpec((None,) + stacked.shape[1:], lambda b, t: (layer,) + (0,) * nd, **kw)


def _whole_spec(a):
    return pl.BlockSpec(a.shape, lambda b, t: (0,) * a.ndim, pipeline_mode=pl.Buffered(1))


def _mixer(x, layer, gmix, w_in, bg, wconv, ghead, wpool, pscale, w_out, w_up, w_down):
    B, T, D = x.shape
    tm = MIX_TM
    steps = B * (T // tm)
    row_spec = pl.BlockSpec((1, tm, D), lambda b, t: (b, t, 0))
    params = (gmix, w_in, bg, wconv, ghead, wpool, pscale, w_out)
    big = (w_in, w_out)

    def slab(w):
        return w.shape[1] // steps

    def cast_spec(w):
        return pl.BlockSpec((None, slab(w), w.shape[2]), lambda b, t: (layer, b * (T // tm) + t, 0))

    def cast_out_spec(w):
        return pl.BlockSpec((slab(w), w.shape[2]), lambda b, t: (b * (T // tm) + t, 0))

    return pl.pallas_call(
        _mixer_kernel,
        grid=(B, T // tm),
        in_specs=([row_spec] + [_layer_spec(a, layer, 1 if any(a is w for w in big) else None) for a in params]
                  + [cast_spec(w_up), cast_spec(w_down)]),
        out_specs=[row_spec, cast_out_spec(w_up), cast_out_spec(w_down)],
        out_shape=[jax.ShapeDtypeStruct(x.shape, x.dtype), jax.ShapeDtypeStruct(w_up.shape[1:], BF16),
                   jax.ShapeDtypeStruct(w_down.shape[1:], BF16)],
        scratch_shapes=[
            pltpu.VMEM((CONV_TAIL, 2 * QK_WIDTH), F32),
            pltpu.VMEM((POOL_TAIL, POOL_WIDTH), F32),
            pltpu.VMEM((HEADS, DK, 2 * DV), F32),
            pltpu.VMEM((SUBLANES, LANES), F32),
            pltpu.VMEM((LANES, tm), F32),
            pltpu.VMEM((D, MAIN_WIDTH + LANES), BF16),
            pltpu.VMEM(w_out.shape[1:], BF16),
        ],
        compiler_params=pltpu.CompilerParams(
            dimension_semantics=("arbitrary", "arbitrary"), vmem_limit_bytes=VMEM_LIMIT),
        name="mixer",
    )(x, *params, w_up, w_down)


def _ffn_kernel(x_ref, gffn_ref, wup_ref, wconv_ref, bconv_ref, wdown_ref, gfinal_ref, out_ref,
                up_tail, act_s, hb_s, *, final_norm):
    tm = FFN_TM
    t = pl.program_id(1)

    @pl.when(t == 0)
    def _():
        up_tail[...] = jnp.zeros_like(up_tail)

    hb_s[...] = _rms(x_ref[0], gffn_ref[...]).astype(BF16)

    def conv(cols, scale=None):
        up = _dot(hb_s[...], wup_ref[:, cols])
        ext = jnp.concatenate([up_tail[:, cols], up], axis=0)
        up_tail[:, cols] = up[tm - CONV_TAIL:, :]
        wc, bias = wconv_ref[:, cols], bconv_ref[:, cols]
        if scale is not None:
            wc, bias = wc * scale, bias * scale
        acc = up * wc[FFN_CONV - 1:FFN_CONV, :] + bias
        for j in range(1, FFN_CONV):
            acc = acc + _shift_rows(ext, j, CONV_TAIL) * wc[FFN_CONV - 1 - j:FFN_CONV - j, :]
        return acc

    for f in range(D_FF // FFN_SUB):
        half_gate = conv(slice(f * FFN_SUB, (f + 1) * FFN_SUB), 0.5)
        val = conv(slice(D_FF + f * FFN_SUB, D_FF + (f + 1) * FFN_SUB))
        act = half_gate * (1.0 + jnp.tanh(half_gate)) * val
        act_s[:, f * FFN_SUB:(f + 1) * FFN_SUB] = act.astype(BF16)

    y = x_ref[0] + _dot(act_s[...], wdown_ref[...])
    if final_norm:
        y = _rms(y, gfinal_ref[...])
    out_ref[0] = y


def _ffn(x, layer, gffn, wup, wconv, bconv, wdown, gfinal, final_norm):
    B, T, D = x.shape
    tm = FFN_TM
    row_spec = pl.BlockSpec((1, tm, D), lambda b, t: (b, t, 0))
    return pl.pallas_call(
        functools.partial(_ffn_kernel, final_norm=final_norm),
        grid=(B, T // tm),
        in_specs=[row_spec, _layer_spec(gffn, layer), _whole_spec(wup), _layer_spec(wconv, layer),
                  _layer_spec(bconv, layer), _whole_spec(wdown), _layer_spec(gfinal, 0)],
        out_specs=row_spec,
        out_shape=jax.ShapeDtypeStruct(x.shape, x.dtype),
        scratch_shapes=[
            pltpu.VMEM((CONV_TAIL, 2 * D_FF), F32),
            pltpu.VMEM((tm, D_FF), BF16),
            pltpu.VMEM((tm, D), BF16),
        ],
        compiler_params=pltpu.CompilerParams(
            dimension_semantics=("arbitrary", "arbitrary"), vmem_limit_bytes=VMEM_LIMIT),
        name="ffn",
    )(x, gffn, wup, wconv, bconv, wdown, gfinal)


def kernel(x, mix_norm, w_in, b_gates, w_qk_conv, head_norm, w_pool, pool_scale, w_out, ffn_norm, w_up,
           w_ffn_conv, b_ffn_conv, w_down, final_norm):
    depth = w_in.shape[0]
    rows = lambda a: a[:, None, :]
    bg = rows(jnp.pad(b_gates, ((0, 0), (0, LANES - GATES))))
    wp = w_pool.astype(BF16).reshape(depth, len(POOL_WINDOWS) // 2, 2, POOL_GROUP, POOL_GROUP)
    zero = jnp.zeros_like(wp[:, :, 0])
    wpool = jnp.concatenate([jnp.concatenate([wp[:, :, 0], zero], axis=-1),
                             jnp.concatenate([zero, wp[:, :, 1]], axis=-1)], axis=-2)
    gmix, ghead, pscale, gffn, bconv = map(rows, (mix_norm, head_norm, pool_scale, ffn_norm, b_ffn_conv))
    gfinal = final_norm.reshape(1, 1, -1)
    for l in range(depth):
        x, wup, wdown = _mixer(x, l, gmix, w_in, bg, w_qk_conv, ghead, wpool, pscale, w_out, w_up, w_down)
        x = _ffn(x, l, gffn, wup, w_ffn_conv, bconv, wdown, gfinal, final_norm=(l == depth - 1))
    return x
```

```python
import functools

import jax
import jax.numpy as jnp
from jax import lax
from jax.experimental import pallas as pl
from jax.experimental.pallas import tpu as pltpu

D_MODEL = 1024
HEADS = 4
DV = 128
DK = 64
QK_WIDTH = HEADS * DK
V_WIDTH = HEADS * DV
QK_CONV = 4
POOL_WINDOWS = (2, 4, 8, 16)
POOL_GROUP = 128
POOL_WIDTH = len(POOL_WINDOWS) * POOL_GROUP
GATES = 2 * HEADS
GATE_LO = 2 * QK_WIDTH + 2 * V_WIDTH
MAIN_WIDTH = GATE_LO + POOL_WIDTH
D_FF = 2816
FFN_CONV = 3
EPS = 1e-6

LANES = 128
SUBLANES = 8
MXU_COLS = 256

MIX_TM = 1024
MIX_GROUP = 256
MIX_CHUNK = 128
FFN_TM = 1024
FFN_SUB = MXU_COLS
POOL_TAIL = 16
CONV_TAIL = SUBLANES
VMEM_LIMIT = 56 * 1024 * 1024

F32 = jnp.float32
BF16 = jnp.bfloat16


def _dot(a, b):
    return jnp.dot(a, b, preferred_element_type=F32)


def _rms(x, g):
    return x * lax.rsqrt(jnp.mean(x * x, axis=-1, keepdims=True) + EPS) * g


def _silu(x):
    return x * jax.nn.sigmoid(x)


def _shift_rows(ext, j, tail):
    return pltpu.roll(ext, j, axis=0)[tail:, :]


def _mixer_kernel(x_ref, gmix_ref, win_t_ref, bg_ref, wconv_ref, ghead_ref, wpool_ref,
                  pscale_ref, wout_f32_ref, wup_f32_ref, wdown_f32_ref, out_ref, wup_ref, wdown_ref,
                  qk_tail, u_tail, c_state, m_state, zt_s, wmain_ref, wout_ref):
    wup_ref[...] = wup_f32_ref[...].astype(BF16)
    wdown_ref[...] = wdown_f32_ref[...].astype(BF16)

    tm, grp, chunk = MIX_TM, MIX_GROUP, MIX_CHUNK
    ngrp, cpg = tm // grp, grp // chunk
    t = pl.program_id(1)

    @pl.when((pl.program_id(0) == 0) & (t == 0))
    def _():
        for dst in range(0, MAIN_WIDTH, MIX_GROUP):
            src = dst if dst < GATE_LO else dst + GATES
            wmain_ref[:, dst:dst + MIX_GROUP] = win_t_ref[src:src + MIX_GROUP, :].T.astype(BF16)
        gates_w = jnp.concatenate([win_t_ref[GATE_LO:GATE_LO + GATES, :],
                                   jnp.zeros((LANES - GATES, D_MODEL), F32)], axis=0)
        wmain_ref[:, MAIN_WIDTH:] = gates_w.T.astype(BF16)
        for rows in (slice(r, r + MIX_GROUP) for r in range(0, D_MODEL, MIX_GROUP)):
            wout_ref[rows, :] = wout_f32_ref[rows, :].astype(BF16)

    @pl.when(t == 0)
    def _():
        qk_tail[...] = jnp.zeros_like(qk_tail)
        u_tail[...] = jnp.zeros_like(u_tail)
        c_state[...] = jnp.zeros_like(c_state)
        m_state[...] = jnp.zeros_like(m_state)

    lane = lax.broadcasted_iota(jnp.int32, (chunk, LANES), 1)
    is_f = (lane >= HEADS) & (lane < GATES)
    causal = (lax.broadcasted_iota(jnp.int32, (chunk, chunk), 0)
              >= lax.broadcasted_iota(jnp.int32, (chunk, chunk), 1))
    tri = causal.astype(BF16)
    wc = wconv_ref[...]
    pscale = pscale_ref[...]
    gmix = gmix_ref[...]
    ghead = ghead_ref[...]
    ones_blk = jnp.ones((chunk, DV), BF16)
    k_pad = jnp.zeros((chunk, MXU_COLS - chunk - DK), BF16)
    rhs_pad = jnp.zeros((MXU_COLS - chunk - DK, 2 * DV), BF16)
    rows_of = lambda c: slice(c * chunk, (c + 1) * chunk)

    def front(r, qk_hist, u_hist):
        rows = slice(r * grp, (r + 1) * grp)
        hb = _rms(x_ref[0, rows, :], gmix).astype(BF16)
        p = _dot(hb, wmain_ref[...])
        gates = p[:, MAIN_WIDTH:] + bg_ref[...]

        qk_pre = p[:, :2 * QK_WIDTH]
        ext = jnp.concatenate([qk_hist, qk_pre], axis=0)
        acc = qk_pre * wc[QK_CONV - 1:QK_CONV, :]
        for j in range(1, QK_CONV):
            acc = acc + _shift_rows(ext, j, CONV_TAIL) * wc[QK_CONV - 1 - j:QK_CONV - j, :]
        qk = _silu(acc)
        f = dict(q=qk[:, :QK_WIDTH].astype(BF16),
                 kt=(qk[:, QK_WIDTH:] * (DK ** -0.5)).T,
                 v=p[:, 2 * QK_WIDTH:2 * QK_WIDTH + V_WIDTH].astype(BF16),
                 o=p[:, 2 * QK_WIDTH + V_WIDTH:2 * QK_WIDTH + 2 * V_WIDTH])

        z_parts = []
        for c in range(cpg):
            g_c = gates[rows_of(c), :]
            lf = jnp.where(is_f, jnp.minimum(g_c, 0.0) - jnp.log(1.0 + jnp.exp(-jnp.abs(g_c))), 0.0)
            hi = lf.astype(BF16)
            r1 = lf - hi.astype(F32)
            mid = r1.astype(BF16)
            lo = (r1 - mid.astype(F32)).astype(BF16)
            b3 = _dot(tri, jnp.concatenate([hi, mid, lo], axis=1))
            b_c = b3[:, :LANES] + b3[:, LANES:2 * LANES] + b3[:, 2 * LANES:]
            z_parts.append(jnp.where(is_f, b_c, g_c))
        f["z"] = jnp.concatenate(z_parts, axis=0)
        zt_s[:, rows] = f["z"].T
        f["g_r"] = {}
        for c in range(cpg):
            lanes = slice(r * grp + c * chunk, r * grp + (c + 1) * chunk)
            for hd in range(HEADS):
                f["g_r"][c, hd] = zt_s[hd:hd + 1, lanes] - zt_s[HEADS + hd:HEADS + hd + 1, lanes]

        u = p[:, 2 * QK_WIDTH + 2 * V_WIDTH:MAIN_WIDTH]
        sums = jnp.concatenate([u_hist, u], axis=0)
        pos = (t * tm + r * grp + 1 + lax.broadcasted_iota(jnp.int32, (grp, 1), 0)).astype(F32)
        ds = []
        for g, w in enumerate(POOL_WINDOWS):
            sums = sums + pltpu.roll(sums, w // 2, axis=0)
            mean = sums[POOL_TAIL:, :POOL_GROUP] * (1.0 / jnp.minimum(pos, float(w)))
            ds.append((mean - u[:, g * POOL_GROUP:(g + 1) * POOL_GROUP]).astype(BF16))
            sums = sums[:, POOL_GROUP:]
        ys = []
        for g2 in range(len(POOL_WINDOWS) // 2):
            cols = slice(2 * g2 * POOL_GROUP, 2 * (g2 + 1) * POOL_GROUP)
            y = _dot(jnp.concatenate(ds[2 * g2:2 * g2 + 2], axis=1), wpool_ref[g2]) * pscale[:, cols]
            ys.append(y.astype(BF16))
        f["pool_out"] = _dot(jnp.concatenate(ys, axis=1), wout_ref[V_WIDTH:, :])
        return f, qk_pre[grp - CONV_TAIL:, :], u[grp - POOL_TAIL:, :]

    def mlstm(r, f, states):
        pairs = [(c, hd) for c in range(cpg) for hd in range(HEADS)]

        st = {}
        for c, hd in pairs:
            g_r = f["g_r"][c, hd]
            bc = jnp.broadcast_to(f["z"][rows_of(c), HEADS + hd:HEADS + hd + 1], (chunk, LANES))
            log_d = jnp.where(causal, bc + g_r, -jnp.inf)
            m_intra = jnp.broadcast_to(jnp.max(log_d, axis=1, keepdims=True), (chunk, LANES))
            g_max = jnp.broadcast_to(jnp.max(g_r, axis=1, keepdims=True), (1, LANES))
            st[c, hd] = dict(g_r=g_r, bc=bc, log_d=log_d, m_intra=m_intra, g_max=g_max)

        for c, hd in pairs:
            d = st[c, hd]
            q_c = f["q"][rows_of(c), hd * DK:(hd + 1) * DK]
            kt_c = f["kt"][hd * DK:(hd + 1) * DK, rows_of(c)]
            dmat = jnp.exp(d["log_d"] - d["m_intra"])
            d["s"] = _dot(q_c, kt_c.astype(BF16)) * dmat
            d["ktw"] = (kt_c * jnp.exp(d["g_r"] - d["g_max"])).astype(BF16)
            d["v_aug"] = jnp.concatenate([f["v"][rows_of(c), hd * DV:(hd + 1) * DV], ones_blk], axis=1)
            d["q_c"] = q_c

        for c, hd in pairs:
            d = st[c, hd]
            d["c_loc"] = _dot(d["ktw"], d["v_aug"])

        new_states = []
        for hd in range(HEADS):
            c_aug, m_prev = states[hd]
            for c in range(cpg):
                d = st[c, hd]
                d["c_prev"], d["m_prev"] = c_aug.astype(BF16), m_prev
                mx = jnp.maximum(m_prev, d["g_max"])
                s_prev, s_loc = jnp.exp(m_prev - mx), jnp.exp(d["g_max"] - mx)
                c_aug = (jnp.concatenate([s_prev, s_prev], axis=1) * c_aug
                         + jnp.concatenate([s_loc, s_loc], axis=1) * d["c_loc"])
                m_prev = d["bc"][chunk - 1:chunk, :] + mx
            new_states.append((c_aug, m_prev))

        for c, hd in pairs:
            d = st[c, hd]
            m_inter = d["bc"] + d["m_prev"]
            m = jnp.maximum(m_inter, d["m_intra"])
            s_w = (jnp.exp(d["m_intra"] - m) * d["s"]).astype(BF16)
            q_w = (jnp.exp(m_inter - m)[:, :DK] * d["q_c"].astype(F32)).astype(BF16)
            lhs = jnp.concatenate([s_w, q_w, k_pad], axis=1)
            rhs = jnp.concatenate([d["v_aug"], d["c_prev"], rhs_pad], axis=0)
            d["tot"], d["m"] = _dot(lhs, rhs), m

        for c in range(cpg):
            ys = []
            for hd in range(HEADS):
                d = st[c, hd]
                den = d["tot"][:, DV:]
                hh = d["tot"][:, :DV] * (1.0 / jnp.maximum(jnp.abs(den), jnp.exp(-d["m"])))
                cols = slice(hd * DV, (hd + 1) * DV)
                y = hh * lax.rsqrt(jnp.mean(hh * hh, axis=-1, keepdims=True) + EPS) * ghead[:, cols]
                ys.append((y * jax.nn.sigmoid(f["o"][rows_of(c), cols])).astype(BF16))
            rows = slice(r * grp + c * chunk, r * grp + (c + 1) * chunk)
            mixed = _dot(jnp.concatenate(ys, axis=1), wout_ref[:V_WIDTH, :]) + f["pool_out"][rows_of(c), :]
            out_ref[0, rows, :] = x_ref[0, rows, :] + mixed
        return new_states

    states = [(c_state[hd], m_state[hd:hd + 1, :]) for hd in range(HEADS)]
    nxt = front(0, qk_tail[...], u_tail[...])
    for r in range(ngrp):
        f, qk_hist, u_hist = nxt
        if r + 1 < ngrp:
            nxt = front(r + 1, qk_hist, u_hist)
        states = mlstm(r, f, states)
    qk_tail[...] = nxt[1]
    u_tail[...] = nxt[2]
    for hd in range(HEADS):
        c_state[hd] = states[hd][0]
        m_state[hd:hd + 1, :] = states[hd][1]


def _layer_spec(stacked, layer, buffers=None):
    nd = stacked.ndim - 1
    kw = {} if buffers is None else dict(pipeline_mode=pl.Buffered(buffers))
    return pl.BlockSpec((None,) + stacked.shape[1:], lambda b, t: (layer,) + (0,) * nd, **kw)


def _whole_spec(a):
    return pl.BlockSpec(a.shape, lambda b, t: (0,) * a.ndim, pipeline_mode=pl.Buffered(1))


def _mixer(x, layer, gmix, w_in_t, bg, wconv, ghead, wpool, pscale, w_out, w_up, w_down):
    B, T, D = x.shape
    tm = MIX_TM
    steps = B * (T // tm)
    row_spec = pl.BlockSpec((1, tm, D), lambda b, t: (b, t, 0))
    params = (gmix, w_in_t, bg, wconv, ghead, wpool, pscale, w_out)
    big = (w_in_t, w_out)

    def slab(w):
        return w.shape[1] // steps

    def cast_spec(w):
        return pl.BlockSpec((None, slab(w), w.shape[2]), lambda b, t: (layer, b * (T // tm) + t, 0))

    def cast_out_spec(w):
        return pl.BlockSpec((slab(w), w.shape[2]), lambda b, t: (b * (T // tm) + t, 0))

    return pl.pallas_call(
        _mixer_kernel,
        grid=(B, T // tm),
        in_specs=([row_spec] + [_layer_spec(a, layer, 1 if any(a is w for w in big) else None) for a in params]
                  + [cast_spec(w_up), cast_spec(w_down)]),
        out_specs=[row_spec, cast_out_spec(w_up), cast_out_spec(w_down)],
        out_shape=[jax.ShapeDtypeStruct(x.shape, x.dtype), jax.ShapeDtypeStruct(w_up.shape[1:], BF16),
                   jax.ShapeDtypeStruct(w_down.shape[1:], BF16)],
        scratch_shapes=[
            pltpu.VMEM((CONV_TAIL, 2 * QK_WIDTH), F32),
            pltpu.VMEM((POOL_TAIL, POOL_WIDTH), F32),
            pltpu.VMEM((HEADS, DK, 2 * DV), F32),
            pltpu.VMEM((SUBLANES, LANES), F32),
            pltpu.VMEM((LANES, tm), F32),
            pltpu.VMEM((D, MAIN_WIDTH + LANES), BF16),
            pltpu.VMEM(w_out.shape[1:], BF16),
        ],
        compiler_params=pltpu.CompilerParams(
            dimension_semantics=("arbitrary", "arbitrary"), vmem_limit_bytes=VMEM_LIMIT),
        name="mixer",
    )(x, *params, w_up, w_down)


def _ffn_kernel(x_ref, gffn_ref, wup_ref, wconv_ref, bconv_ref, wdown_ref, gfinal_ref, out_ref,
                up_tail, act_s, hb_s, *, final_norm):
    tm = FFN_TM
    t = pl.program_id(1)

    @pl.when(t == 0)
    def _():
        up_tail[...] = jnp.zeros_like(up_tail)

    hb_s[...] = _rms(x_ref[0], gffn_ref[...]).astype(BF16)

    def conv(cols, scale=None):
        up = _dot(hb_s[...], wup_ref[:, cols])
        ext = jnp.concatenate([up_tail[:, cols], up], axis=0)
        up_tail[:, cols] = up[tm - CONV_TAIL:, :]
        wc, bias = wconv_ref[:, cols], bconv_ref[:, cols]
        if scale is not None:
            wc, bias = wc * scale, bias * scale
        acc = up * wc[FFN_CONV - 1:FFN_CONV, :] + bias
        for j in range(1, FFN_CONV):
            acc = acc + _shift_rows(ext, j, CONV_TAIL) * wc[FFN_CONV - 1 - j:FFN_CONV - j, :]
        return acc

    for f in range(D_FF // FFN_SUB):
        half_gate = conv(slice(f * FFN_SUB, (f + 1) * FFN_SUB), 0.5)
        val = conv(slice(D_FF + f * FFN_SUB, D_FF + (f + 1) * FFN_SUB))
        act = half_gate * (1.0 + jnp.tanh(half_gate)) * val
        act_s[:, f * FFN_SUB:(f + 1) * FFN_SUB] = act.astype(BF16)

    y = x_ref[0] + _dot(act_s[...], wdown_ref[...])
    if final_norm:
        y = _rms(y, gfinal_ref[...])
    out_ref[0] = y


def _ffn(x, layer, gffn, wup, wconv, bconv, wdown, gfinal, final_norm):
    B, T, D = x.shape
    tm = FFN_TM
    row_spec = pl.BlockSpec((1, tm, D), lambda b, t: (b, t, 0))
    return pl.pallas_call(
        functools.partial(_ffn_kernel, final_norm=final_norm),
        grid=(B, T // tm),
        in_specs=[row_spec, _layer_spec(gffn, layer), _whole_spec(wup), _layer_spec(wconv, layer),
                  _layer_spec(bconv, layer), _whole_spec(wdown), _layer_spec(gfinal, 0)],
        out_specs=row_spec,
        out_shape=jax.ShapeDtypeStruct(x.shape, x.dtype),
        scratch_shapes=[
            pltpu.VMEM((CONV_TAIL, 2 * D_FF), F32),
            pltpu.VMEM((tm, D_FF), BF16),
            pltpu.VMEM((tm, D), BF16),
        ],
        compiler_params=pltpu.CompilerParams(
            dimension_semantics=("arbitrary", "arbitrary"), vmem_limit_bytes=VMEM_LIMIT),
        name="ffn",
    )(x, gffn, wup, wconv, bconv, wdown, gfinal)


def kernel(x, mix_norm, w_in, b_gates, w_qk_conv, head_norm, w_pool, pool_scale, w_out, ffn_norm, w_up,
           w_ffn_conv, b_ffn_conv, w_down, final_norm):
    depth = w_in.shape[0]
    rows = lambda a: a[:, None, :]
    bg = rows(jnp.pad(b_gates, ((0, 0), (0, LANES - GATES))))
    w_in_t = jnp.swapaxes(w_in, 1, 2)
    wp = w_pool.astype(BF16).reshape(depth, len(POOL_WINDOWS) // 2, 2, POOL_GROUP, POOL_GROUP)
    zero = jnp.zeros_like(wp[:, :, 0])
    wpool = jnp.concatenate([jnp.concatenate([wp[:, :, 0], zero], axis=-1),
                             jnp.concatenate([zero, wp[:, :, 1]], axis=-1)], axis=-2)
    gmix, ghead, pscale, gffn, bconv = map(rows, (mix_norm, head_norm, pool_scale, ffn_norm, b_ffn_conv))
    gfinal = final_norm.reshape(1, 1, -1)
    for l in range(depth):
        x, wup, wdown = _mixer(x, l, gmix, w_in_t, bg, w_qk_conv, ghead, wpool, pscale, w_out, w_up, w_down)
        x = _ffn(x, l, gffn, wup, w_ffn_conv, bconv, wdown, gfinal, final_norm=(l == depth - 1))
    return x
```

```python
import functools

import jax
import jax.numpy as jnp
from jax import lax
from jax.experimental import pallas as pl
from jax.experimental.pallas import tpu as pltpu

D_MODEL = 1024
HEADS = 4
DV = 128
DK = 64
QK_WIDTH = HEADS * DK
V_WIDTH = HEADS * DV
QK_CONV = 4
POOL_WINDOWS = (2, 4, 8, 16)
POOL_GROUP = 128
POOL_WIDTH = len(POOL_WINDOWS) * POOL_GROUP
GATES = 2 * HEADS
GATE_LO = 2 * QK_WIDTH + 2 * V_WIDTH
MAIN_WIDTH = GATE_LO + POOL_WIDTH
D_FF = 2816
FFN_CONV = 3
EPS = 1e-6

LANES = 128
SUBLANES = 8
MXU_COLS = 256

MIX_TM = 1024
MIX_GROUP = 256
MIX_CHUNK = 128
FFN_TM = 1024
FFN_SUB = MXU_COLS
POOL_TAIL = 16
CONV_TAIL = SUBLANES
VMEM_LIMIT = 56 * 1024 * 1024

F32 = jnp.float32
BF16 = jnp.bfloat16


def _dot(a, b):
    return jnp.dot(a, b, preferred_element_type=F32)


def _rms(x, g):
    return x * lax.rsqrt(jnp.mean(x * x, axis=-1, keepdims=True) + EPS) * g


def _silu(x):
    return x * jax.nn.sigmoid(x)


def _shift_rows(ext, j, tail):
    return pltpu.roll(ext, j, axis=0)[tail:, :]


def _mixer_kernel(x_ref, gmix_ref, win_t_ref, bg_ref, wconv_ref, ghead_ref, wpool_f32_ref,
                  pscale_ref, wout_f32_ref, wup_f32_ref, wdown_f32_ref, out_ref, wup_ref, wdown_ref,
                  qk_tail, u_tail, c_state, m_state, zt_s, wmain_ref, wout_ref, wpool_ref, *, layer):
    wup_ref[...] = wup_f32_ref[...].astype(BF16)
    wdown_ref[...] = wdown_f32_ref[...].astype(BF16)

    tm, grp, chunk = MIX_TM, MIX_GROUP, MIX_CHUNK
    ngrp, cpg = tm // grp, grp // chunk
    t = pl.program_id(1)

    @pl.when((pl.program_id(0) == 0) & (t == 0))
    def _():
        for dst in range(0, MAIN_WIDTH, MIX_GROUP):
            src = dst if dst < GATE_LO else dst + GATES
            wmain_ref[:, dst:dst + MIX_GROUP] = win_t_ref[src:src + MIX_GROUP, :].T.astype(BF16)
        gates_w = jnp.concatenate([win_t_ref[GATE_LO:GATE_LO + GATES, :],
                                   jnp.zeros((LANES - GATES, D_MODEL), F32)], axis=0)
        wmain_ref[:, MAIN_WIDTH:] = gates_w.T.astype(BF16)
        for rows in (slice(r, r + MIX_GROUP) for r in range(0, D_MODEL, MIX_GROUP)):
            wout_ref[rows, :] = wout_f32_ref[rows, :].astype(BF16)
        wpool_ref[...] = jnp.zeros_like(wpool_ref)
        for g in range(len(POOL_WINDOWS)):
            lo = (g % 2) * POOL_GROUP
            wpool_ref[g // 2, lo:lo + POOL_GROUP, lo:lo + POOL_GROUP] = wpool_f32_ref[g].astype(BF16)

    @pl.when(t == 0)
    def _():
        qk_tail[...] = jnp.zeros_like(qk_tail)
        u_tail[...] = jnp.zeros_like(u_tail)
        c_state[...] = jnp.zeros_like(c_state)
        m_state[...] = jnp.zeros_like(m_state)

    lane = lax.broadcasted_iota(jnp.int32, (chunk, LANES), 1)
    is_f = (lane >= HEADS) & (lane < GATES)
    causal = (lax.broadcasted_iota(jnp.int32, (chunk, chunk), 0)
              >= lax.broadcasted_iota(jnp.int32, (chunk, chunk), 1))
    tri = causal.astype(BF16)
    wc = wconv_ref[...]
    row = slice(layer, layer + 1)
    pscale = pscale_ref[row, :]
    gmix = gmix_ref[row, :]
    ghead = ghead_ref[row, :]
    bg = bg_ref[row, :]
    ones_blk = jnp.ones((chunk, DV), BF16)
    k_pad = jnp.zeros((chunk, MXU_COLS - chunk - DK), BF16)
    rhs_pad = jnp.zeros((MXU_COLS - chunk - DK, 2 * DV), BF16)
    rows_of = lambda c: slice(c * chunk, (c + 1) * chunk)

    def front(r, qk_hist, u_hist):
        rows = slice(r * grp, (r + 1) * grp)
        hb = _rms(x_ref[0, rows, :], gmix).astype(BF16)
        p = _dot(hb, wmain_ref[...])
        gates = p[:, MAIN_WIDTH:] + bg

        qk_pre = p[:, :2 * QK_WIDTH]
        ext = jnp.concatenate([qk_hist, qk_pre], axis=0)
        acc = qk_pre * wc[QK_CONV - 1:QK_CONV, :]
        for j in range(1, QK_CONV):
            acc = acc + _shift_rows(ext, j, CONV_TAIL) * wc[QK_CONV - 1 - j:QK_CONV - j, :]
        qk = _silu(acc)
        f = dict(q=qk[:, :QK_WIDTH].astype(BF16),
                 kt=(qk[:, QK_WIDTH:] * (DK ** -0.5)).T,
                 v=p[:, 2 * QK_WIDTH:2 * QK_WIDTH + V_WIDTH].astype(BF16),
                 o=p[:, 2 * QK_WIDTH + V_WIDTH:2 * QK_WIDTH + 2 * V_WIDTH])

        z_parts = []
        for c in range(cpg):
            g_c = gates[rows_of(c), :]
            lf = jnp.where(is_f, jnp.minimum(g_c, 0.0) - jnp.log(1.0 + jnp.exp(-jnp.abs(g_c))), 0.0)
            hi = lf.astype(BF16)
            r1 = lf - hi.astype(F32)
            mid = r1.astype(BF16)
            lo = (r1 - mid.astype(F32)).astype(BF16)
            b3 = _dot(tri, jnp.concatenate([hi, mid, lo], axis=1))
            b_c = b3[:, :LANES] + b3[:, LANES:2 * LANES] + b3[:, 2 * LANES:]
            z_parts.append(jnp.where(is_f, b_c, g_c))
        f["z"] = jnp.concatenate(z_parts, axis=0)
        zt_s[:, rows] = f["z"].T
        f["g_r"] = {}
        for c in range(cpg):
            lanes = slice(r * grp + c * chunk, r * grp + (c + 1) * chunk)
            for hd in range(HEADS):
                f["g_r"][c, hd] = zt_s[hd:hd + 1, lanes] - zt_s[HEADS + hd:HEADS + hd + 1, lanes]

        u = p[:, 2 * QK_WIDTH + 2 * V_WIDTH:MAIN_WIDTH]
        sums = jnp.concatenate([u_hist, u], axis=0)
        pos = (t * tm + r * grp + 1 + lax.broadcasted_iota(jnp.int32, (grp, 1), 0)).astype(F32)
        ds = []
        for g, w in enumerate(POOL_WINDOWS):
            sums = sums + pltpu.roll(sums, w // 2, axis=0)
            mean = sums[POOL_TAIL:, :POOL_GROUP] * (1.0 / jnp.minimum(pos, float(w)))
            ds.append((mean - u[:, g * POOL_GROUP:(g + 1) * POOL_GROUP]).astype(BF16))
            sums = sums[:, POOL_GROUP:]
        ys = []
        for g2 in range(len(POOL_WINDOWS) // 2):
            cols = slice(2 * g2 * POOL_GROUP, 2 * (g2 + 1) * POOL_GROUP)
            y = _dot(jnp.concatenate(ds[2 * g2:2 * g2 + 2], axis=1), wpool_ref[g2]) * pscale[:, cols]
            ys.append(y.astype(BF16))
        f["pool_out"] = _dot(jnp.concatenate(ys, axis=1), wout_ref[V_WIDTH:, :])
        return f, qk_pre[grp - CONV_TAIL:, :], u[grp - POOL_TAIL:, :]

    def mlstm(r, f, states):
        pairs = [(c, hd) for c in range(cpg) for hd in range(HEADS)]

        st = {}
        for c, hd in pairs:
            g_r = f["g_r"][c, hd]
            bc = jnp.broadcast_to(f["z"][rows_of(c), HEADS + hd:HEADS + hd + 1], (chunk, LANES))
            log_d = jnp.where(causal, bc + g_r, -jnp.inf)
            m_intra = jnp.broadcast_to(jnp.max(log_d, axis=1, keepdims=True), (chunk, LANES))
            g_max = jnp.broadcast_to(jnp.max(g_r, axis=1, keepdims=True), (1, LANES))
            st[c, hd] = dict(g_r=g_r, bc=bc, log_d=log_d, m_intra=m_intra, g_max=g_max)

        for c, hd in pairs:
            d = st[c, hd]
            q_c = f["q"][rows_of(c), hd * DK:(hd + 1) * DK]
            kt_c = f["kt"][hd * DK:(hd + 1) * DK, rows_of(c)]
            dmat = jnp.exp(d["log_d"] - d["m_intra"])
            d["s"] = _dot(q_c, kt_c.astype(BF16)) * dmat
            d["ktw"] = (kt_c * jnp.exp(d["g_r"] - d["g_max"])).astype(BF16)
            d["v_aug"] = jnp.concatenate([f["v"][rows_of(c), hd * DV:(hd + 1) * DV], ones_blk], axis=1)
            d["q_c"] = q_c

        for c, hd in pairs:
            d = st[c, hd]
            d["c_loc"] = _dot(d["ktw"], d["v_aug"])

        new_states = []
        for hd in range(HEADS):
            c_aug, m_prev = states[hd]
            for c in range(cpg):
                d = st[c, hd]
                d["c_prev"], d["m_prev"] = c_aug.astype(BF16), m_prev
                mx = jnp.maximum(m_prev, d["g_max"])
                s_prev, s_loc = jnp.exp(m_prev - mx), jnp.exp(d["g_max"] - mx)
                c_aug = (jnp.concatenate([s_prev, s_prev], axis=1) * c_aug
                         + jnp.concatenate([s_loc, s_loc], axis=1) * d["c_loc"])
                m_prev = d["bc"][chunk - 1:chunk, :] + mx
            new_states.append((c_aug, m_prev))

        for c, hd in pairs:
            d = st[c, hd]
            m_inter = d["bc"] + d["m_prev"]
            m = jnp.maximum(m_inter, d["m_intra"])
            s_w = (jnp.exp(d["m_intra"] - m) * d["s"]).astype(BF16)
            q_w = (jnp.exp(m_inter - m)[:, :DK] * d["q_c"].astype(F32)).astype(BF16)
            lhs = jnp.concatenate([s_w, q_w, k_pad], axis=1)
            rhs = jnp.concatenate([d["v_aug"], d["c_prev"], rhs_pad], axis=0)
            d["tot"], d["m"] = _dot(lhs, rhs), m

        for c in range(cpg):
            ys = []
            for hd in range(HEADS):
                d = st[c, hd]
                den = d["tot"][:, DV:]
                hh = d["tot"][:, :DV] * (1.0 / jnp.maximum(jnp.abs(den), jnp.exp(-d["m"])))
                cols = slice(hd * DV, (hd + 1) * DV)
                y = hh * lax.rsqrt(jnp.mean(hh * hh, axis=-1, keepdims=True) + EPS) * ghead[:, cols]
                ys.append((y * jax.nn.sigmoid(f["o"][rows_of(c), cols])).astype(BF16))
            rows = slice(r * grp + c * chunk, r * grp + (c + 1) * chunk)
            mixed = _dot(jnp.concatenate(ys, axis=1), wout_ref[:V_WIDTH, :]) + f["pool_out"][rows_of(c), :]
            out_ref[0, rows, :] = x_ref[0, rows, :] + mixed
        return new_states

    states = [(c_state[hd], m_state[hd:hd + 1, :]) for hd in range(HEADS)]
    nxt = front(0, qk_tail[...], u_tail[...])
    for r in range(ngrp):
        f, qk_hist, u_hist = nxt
        if r + 1 < ngrp:
            nxt = front(r + 1, qk_hist, u_hist)
        states = mlstm(r, f, states)
    qk_tail[...] = nxt[1]
    u_tail[...] = nxt[2]
    for hd in range(HEADS):
        c_state[hd] = states[hd][0]
        m_state[hd:hd + 1, :] = states[hd][1]


def _layer_spec(stacked, layer, buffers=None):
    nd = stacked.ndim - 1
    kw = {} if buffers is None else dict(pipeline_mode=pl.Buffered(buffers))
    return pl.BlockSpec((None,) + stacked.shape[1:], lambda b, t: (layer,) + (0,) * nd, **kw)


def _whole_spec(a):
    return pl.BlockSpec(a.shape, lambda b, t: (0,) * a.ndim, pipeline_mode=pl.Buffered(1))


def _mixer(x, layer, gmix, w_in_t, bg, wconv, ghead, wpool, pscale, w_out, w_up, w_down):
    B, T, D = x.shape
    tm = MIX_TM
    steps = B * (T // tm)
    row_spec = pl.BlockSpec((1, tm, D), lambda b, t: (b, t, 0))
    params = (gmix, w_in_t, bg, wconv, ghead, wpool, pscale, w_out)
    big = (w_in_t, w_out)

    def param_spec(a):
        if a.ndim == 2:
            return pl.BlockSpec(a.shape, lambda b, t: (0, 0))
        return _layer_spec(a, layer, 1 if any(a is w for w in big) else None)

    def slab(w):
        return w.shape[1] // steps

    def cast_spec(w):
        return pl.BlockSpec((None, slab(w), w.shape[2]), lambda b, t: (layer, b * (T // tm) + t, 0))

    def cast_out_spec(w):
        return pl.BlockSpec((slab(w), w.shape[2]), lambda b, t: (b * (T // tm) + t, 0))

    return pl.pallas_call(
        functools.partial(_mixer_kernel, layer=layer),
        grid=(B, T // tm),
        in_specs=[row_spec] + [param_spec(a) for a in params] + [cast_spec(w_up), cast_spec(w_down)],
        out_specs=[row_spec, cast_out_spec(w_up), cast_out_spec(w_down)],
        out_shape=[jax.ShapeDtypeStruct(x.shape, x.dtype), jax.ShapeDtypeStruct(w_up.shape[1:], BF16),
                   jax.ShapeDtypeStruct(w_down.shape[1:], BF16)],
        scratch_shapes=[
            pltpu.VMEM((CONV_TAIL, 2 * QK_WIDTH), F32),
            pltpu.VMEM((POOL_TAIL, POOL_WIDTH), F32),
            pltpu.VMEM((HEADS, DK, 2 * DV), F32),
            pltpu.VMEM((SUBLANES, LANES), F32),
            pltpu.VMEM((LANES, tm), F32),
            pltpu.VMEM((D, MAIN_WIDTH + LANES), BF16),
            pltpu.VMEM(w_out.shape[1:], BF16),
            pltpu.VMEM((len(POOL_WINDOWS) // 2, 2 * POOL_GROUP, 2 * POOL_GROUP), BF16),
        ],
        compiler_params=pltpu.CompilerParams(
            dimension_semantics=("arbitrary", "arbitrary"), vmem_limit_bytes=VMEM_LIMIT),
        name="mixer",
    )(x, *params, w_up, w_down)


def _ffn_kernel(x_ref, gffn_ref, wup_ref, wconv_ref, bconv_ref, wdown_ref, gfinal_ref, out_ref,
                up_tail, act_s, hb_s, *, layer, final_norm):
    tm = FFN_TM
    t = pl.program_id(1)

    @pl.when(t == 0)
    def _():
        up_tail[...] = jnp.zeros_like(up_tail)

    row = slice(layer, layer + 1)
    hb_s[...] = _rms(x_ref[0], gffn_ref[row, :]).astype(BF16)

    def conv(cols, scale=None):
        up = _dot(hb_s[...], wup_ref[:, cols])
        ext = jnp.concatenate([up_tail[:, cols], up], axis=0)
        up_tail[:, cols] = up[tm - CONV_TAIL:, :]
        wc, bias = wconv_ref[:, cols], bconv_ref[row, cols]
        if scale is not None:
            wc, bias = wc * scale, bias * scale
        acc = up * wc[FFN_CONV - 1:FFN_CONV, :] + bias
        for j in range(1, FFN_CONV):
            acc = acc + _shift_rows(ext, j, CONV_TAIL) * wc[FFN_CONV - 1 - j:FFN_CONV - j, :]
        return acc

    for f in range(D_FF // FFN_SUB):
        half_gate = conv(slice(f * FFN_SUB, (f + 1) * FFN_SUB), 0.5)
        val = conv(slice(D_FF + f * FFN_SUB, D_FF + (f + 1) * FFN_SUB))
        act = half_gate * (1.0 + jnp.tanh(half_gate)) * val
        act_s[:, f * FFN_SUB:(f + 1) * FFN_SUB] = act.astype(BF16)

    y = x_ref[0] + _dot(act_s[...], wdown_ref[...])
    if final_norm:
        y = _rms(y, gfinal_ref[...])
    out_ref[0] = y


def _ffn(x, layer, gffn, wup, wconv, bconv, wdown, gfinal, final_norm):
    B, T, D = x.shape
    tm = FFN_TM
    row_spec = pl.BlockSpec((1, tm, D), lambda b, t: (b, t, 0))
    small = lambda a: pl.BlockSpec(a.shape, lambda b, t: (0, 0))
    return pl.pallas_call(
        functools.partial(_ffn_kernel, layer=layer, final_norm=final_norm),
        grid=(B, T // tm),
        in_specs=[row_spec, small(gffn), _whole_spec(wup), _layer_spec(wconv, layer),
                  small(bconv), _whole_spec(wdown), small(gfinal)],
        out_specs=row_spec,
        out_shape=jax.ShapeDtypeStruct(x.shape, x.dtype),
        scratch_shapes=[
            pltpu.VMEM((CONV_TAIL, 2 * D_FF), F32),
            pltpu.VMEM((tm, D_FF), BF16),
            pltpu.VMEM((tm, D), BF16),
        ],
        compiler_params=pltpu.CompilerParams(
            dimension_semantics=("arbitrary", "arbitrary"), vmem_limit_bytes=VMEM_LIMIT),
        name="ffn",
    )(x, gffn, wup, wconv, bconv, wdown, gfinal)


def kernel(x, mix_norm, w_in, b_gates, w_qk_conv, head_norm, w_pool, pool_scale, w_out, ffn_norm, w_up,
           w_ffn_conv, b_ffn_conv, w_down, final_norm):
    depth = w_in.shape[0]
    bg = jnp.pad(b_gates, ((0, 0), (0, LANES - GATES)))
    w_in_t = jnp.swapaxes(w_in, 1, 2)
    gfinal = final_norm.reshape(1, -1)
    for l in range(depth):
        x, wup, wdown = _mixer(x, l, mix_norm, w_in_t, bg, w_qk_conv, head_norm, w_pool, pool_scale, w_out,
                               w_up, w_down)
        x = _ffn(x, l, ffn_norm, wup, w_ffn_conv, b_ffn_conv, wdown, gfinal, final_norm=(l == depth - 1))
    return x
```

```python
import functools

import jax
import jax.numpy as jnp
from jax import lax
from jax.experimental import pallas as pl
from jax.experimental.pallas import tpu as pltpu

D_MODEL = 1024
HEADS = 4
DV = 128
DK = 64
QK_WIDTH = HEADS * DK
V_WIDTH = HEADS * DV
QK_CONV = 4
POOL_WINDOWS = (2, 4, 8, 16)
POOL_GROUP = 128
POOL_WIDTH = len(POOL_WINDOWS) * POOL_GROUP
GATES = 2 * HEADS
GATE_LO = 2 * QK_WIDTH + 2 * V_WIDTH
MAIN_WIDTH = GATE_LO + POOL_WIDTH
D_FF = 2816
FFN_CONV = 3
EPS = 1e-6

LANES = 128
SUBLANES = 8
MXU_COLS = 256

MIX_TM = 1024
MIX_GROUP = 256
MIX_CHUNK = 128
FFN_TM = 1024
FFN_SUB = MXU_COLS
PERM_GROUP = SUBLANES * SUBLANES
POOL_TAIL = 16
CONV_TAIL = SUBLANES
VMEM_LIMIT = 56 * 1024 * 1024

F32 = jnp.float32
BF16 = jnp.bfloat16


def _dot(a, b):
    return jnp.dot(a, b, preferred_element_type=F32)


def _rms(x, g):
    return x * lax.rsqrt(jnp.mean(x * x, axis=-1, keepdims=True) + EPS) * g


def _silu(x):
    return x * jax.nn.sigmoid(x)


def _shift_rows(ext, j, tail):
    return pltpu.roll(ext, j, axis=0)[tail:, :]


def _mixer_kernel(x_ref, gmix_ref, win_t_ref, bg_ref, wconv_ref, ghead_ref, wpool_f32_ref,
                  pscale_ref, wout_f32_ref, wup_f32_ref, wdown_f32_ref, out_ref, wup_ref, wdown_ref,
                  qk_tail, u_tail, c_state, m_state, zt_s, wmain_ref, wout_ref, wpool_ref, *, layer):
    wup_ref[...] = wup_f32_ref[...].astype(BF16)
    wdown_ref[...] = wdown_f32_ref[...].astype(BF16)

    tm, grp, chunk = MIX_TM, MIX_GROUP, MIX_CHUNK
    ngrp, cpg = tm // grp, grp // chunk
    t = pl.program_id(1)

    @pl.when((pl.program_id(0) == 0) & (t == 0))
    def _():
        for dst in range(0, MAIN_WIDTH, MIX_GROUP):
            src = dst if dst < GATE_LO else dst + GATES
            wmain_ref[:, dst:dst + MIX_GROUP] = win_t_ref[src:src + MIX_GROUP, :].T.astype(BF16)
        gates_w = jnp.concatenate([win_t_ref[GATE_LO:GATE_LO + GATES, :],
                                   jnp.zeros((LANES - GATES, D_MODEL), F32)], axis=0)
        wmain_ref[:, MAIN_WIDTH:] = gates_w.T.astype(BF16)
        for rows in (slice(r, r + MIX_GROUP) for r in range(0, D_MODEL, MIX_GROUP)):
            wout_ref[rows, :] = wout_f32_ref[rows, :].astype(BF16)
        wpool_ref[...] = jnp.zeros_like(wpool_ref)
        for g in range(len(POOL_WINDOWS)):
            lo = (g % 2) * POOL_GROUP
            wpool_ref[g // 2, lo:lo + POOL_GROUP, lo:lo + POOL_GROUP] = wpool_f32_ref[g].astype(BF16)

    @pl.when(t == 0)
    def _():
        qk_tail[...] = jnp.zeros_like(qk_tail)
        u_tail[...] = jnp.zeros_like(u_tail)
        c_state[...] = jnp.zeros_like(c_state)
        m_state[...] = jnp.zeros_like(m_state)

    lane = lax.broadcasted_iota(jnp.int32, (chunk, LANES), 1)
    is_f = (lane >= HEADS) & (lane < GATES)
    causal = (lax.broadcasted_iota(jnp.int32, (chunk, chunk), 0)
              >= lax.broadcasted_iota(jnp.int32, (chunk, chunk), 1))
    tri = causal.astype(BF16)
    wc = wconv_ref[...]
    row = slice(layer, layer + 1)
    pscale = pscale_ref[row, :]
    gmix = gmix_ref[row, :]
    ghead = ghead_ref[row, :]
    bg = bg_ref[row, :]
    ones_blk = jnp.ones((chunk, DV), BF16)
    k_pad = jnp.zeros((chunk, MXU_COLS - chunk - DK), BF16)
    rhs_pad = jnp.zeros((MXU_COLS - chunk - DK, 2 * DV), BF16)
    rows_of = lambda c: slice(c * chunk, (c + 1) * chunk)

    def front(r, qk_hist, u_hist):
        rows = slice(r * grp, (r + 1) * grp)
        hb = _rms(x_ref[0, rows, :], gmix).astype(BF16)
        p = _dot(hb, wmain_ref[...])
        gates = p[:, MAIN_WIDTH:] + bg

        qk_pre = p[:, :2 * QK_WIDTH]
        ext = jnp.concatenate([qk_hist, qk_pre], axis=0)
        acc = qk_pre * wc[QK_CONV - 1:QK_CONV, :]
        for j in range(1, QK_CONV):
            acc = acc + _shift_rows(ext, j, CONV_TAIL) * wc[QK_CONV - 1 - j:QK_CONV - j, :]
        qk = _silu(acc)
        f = dict(q=qk[:, :QK_WIDTH].astype(BF16),
                 kt=(qk[:, QK_WIDTH:] * (DK ** -0.5)).T,
                 v=p[:, 2 * QK_WIDTH:2 * QK_WIDTH + V_WIDTH].astype(BF16),
                 o=p[:, 2 * QK_WIDTH + V_WIDTH:2 * QK_WIDTH + 2 * V_WIDTH])

        z_parts = []
        for c in range(cpg):
            g_c = gates[rows_of(c), :]
            lf = jnp.where(is_f, jnp.minimum(g_c, 0.0) - jnp.log(1.0 + jnp.exp(-jnp.abs(g_c))), 0.0)
            hi = lf.astype(BF16)
            r1 = lf - hi.astype(F32)
            mid = r1.astype(BF16)
            lo = (r1 - mid.astype(F32)).astype(BF16)
            b3 = _dot(tri, jnp.concatenate([hi, mid, lo], axis=1))
            b_c = b3[:, :LANES] + b3[:, LANES:2 * LANES] + b3[:, 2 * LANES:]
            z_parts.append(jnp.where(is_f, b_c, g_c))
        f["z"] = jnp.concatenate(z_parts, axis=0)
        zt_s[:, rows] = f["z"].T
        f["g_r"] = {}
        for c in range(cpg):
            lanes = slice(r * grp + c * chunk, r * grp + (c + 1) * chunk)
            for hd in range(HEADS):
                f["g_r"][c, hd] = zt_s[hd:hd + 1, lanes] - zt_s[HEADS + hd:HEADS + hd + 1, lanes]

        u = p[:, 2 * QK_WIDTH + 2 * V_WIDTH:MAIN_WIDTH]
        sums = jnp.concatenate([u_hist, u], axis=0)
        pos = (t * tm + r * grp + 1 + lax.broadcasted_iota(jnp.int32, (grp, 1), 0)).astype(F32)
        ds = []
        for g, w in enumerate(POOL_WINDOWS):
            sums = sums + pltpu.roll(sums, w // 2, axis=0)
            mean = sums[POOL_TAIL:, :POOL_GROUP] * (1.0 / jnp.minimum(pos, float(w)))
            ds.append((mean - u[:, g * POOL_GROUP:(g + 1) * POOL_GROUP]).astype(BF16))
            sums = sums[:, POOL_GROUP:]
        ys = []
        for g2 in range(len(POOL_WINDOWS) // 2):
            cols = slice(2 * g2 * POOL_GROUP, 2 * (g2 + 1) * POOL_GROUP)
            y = _dot(jnp.concatenate(ds[2 * g2:2 * g2 + 2], axis=1), wpool_ref[g2]) * pscale[:, cols]
            ys.append(y.astype(BF16))
        f["pool_out"] = _dot(jnp.concatenate(ys, axis=1), wout_ref[V_WIDTH:, :])
        return f, qk_pre[grp - CONV_TAIL:, :], u[grp - POOL_TAIL:, :]

    def mlstm(r, f, states):
        pairs = [(c, hd) for c in range(cpg) for hd in range(HEADS)]

        st = {}
        for c, hd in pairs:
            g_r = f["g_r"][c, hd]
            bc = jnp.broadcast_to(f["z"][rows_of(c), HEADS + hd:HEADS + hd + 1], (chunk, LANES))
            log_d = jnp.where(causal, bc + g_r, -jnp.inf)
            m_intra = jnp.broadcast_to(jnp.max(log_d, axis=1, keepdims=True), (chunk, LANES))
            g_max = jnp.broadcast_to(jnp.max(g_r, axis=1, keepdims=True), (1, LANES))
            st[c, hd] = dict(g_r=g_r, bc=bc, log_d=log_d, m_intra=m_intra, g_max=g_max)

        for c, hd in pairs:
            d = st[c, hd]
            q_c = f["q"][rows_of(c), hd * DK:(hd + 1) * DK]
            kt_c = f["kt"][hd * DK:(hd + 1) * DK, rows_of(c)]
            dmat = jnp.exp(d["log_d"] - d["m_intra"])
            d["s"] = _dot(q_c, kt_c.astype(BF16)) * dmat
            d["ktw"] = (kt_c * jnp.exp(d["g_r"] - d["g_max"])).astype(BF16)
            d["v_aug"] = jnp.concatenate([f["v"][rows_of(c), hd * DV:(hd + 1) * DV], ones_blk], axis=1)
            d["q_c"] = q_c

        for c, hd in pairs:
            d = st[c, hd]
            d["c_loc"] = _dot(d["ktw"], d["v_aug"])

        new_states = []
        for hd in range(HEADS):
            c_aug, m_prev = states[hd]
            for c in range(cpg):
                d = st[c, hd]
                d["c_prev"], d["m_prev"] = c_aug.astype(BF16), m_prev
                mx = jnp.maximum(m_prev, d["g_max"])
                s_prev, s_loc = jnp.exp(m_prev - mx), jnp.exp(d["g_max"] - mx)
                c_aug = (jnp.concatenate([s_prev, s_prev], axis=1) * c_aug
                         + jnp.concatenate([s_loc, s_loc], axis=1) * d["c_loc"])
                m_prev = d["bc"][chunk - 1:chunk, :] + mx
            new_states.append((c_aug, m_prev))

        for c, hd in pairs:
            d = st[c, hd]
            m_inter = d["bc"] + d["m_prev"]
            m = jnp.maximum(m_inter, d["m_intra"])
            s_w = (jnp.exp(d["m_intra"] - m) * d["s"]).astype(BF16)
            q_w = (jnp.exp(m_inter - m)[:, :DK] * d["q_c"].astype(F32)).astype(BF16)
            lhs = jnp.concatenate([s_w, q_w, k_pad], axis=1)
            rhs = jnp.concatenate([d["v_aug"], d["c_prev"], rhs_pad], axis=0)
            d["tot"], d["m"] = _dot(lhs, rhs), m

        for c in range(cpg):
            ys = []
            for hd in range(HEADS):
                d = st[c, hd]
                den = d["tot"][:, DV:]
                hh = d["tot"][:, :DV] * (1.0 / jnp.maximum(jnp.abs(den), jnp.exp(-d["m"])))
                cols = slice(hd * DV, (hd + 1) * DV)
                y = hh * lax.rsqrt(jnp.mean(hh * hh, axis=-1, keepdims=True) + EPS) * ghead[:, cols]
                ys.append((y * jax.nn.sigmoid(f["o"][rows_of(c), cols])).astype(BF16))
            rows = slice(r * grp + c * chunk, r * grp + (c + 1) * chunk)
            mixed = _dot(jnp.concatenate(ys, axis=1), wout_ref[:V_WIDTH, :]) + f["pool_out"][rows_of(c), :]
            out_ref[0, rows, :] = x_ref[0, rows, :] + mixed
        return new_states

    states = [(c_state[hd], m_state[hd:hd + 1, :]) for hd in range(HEADS)]
    nxt = front(0, qk_tail[...], u_tail[...])
    for r in range(ngrp):
        f, qk_hist, u_hist = nxt
        if r + 1 < ngrp:
            nxt = front(r + 1, qk_hist, u_hist)
        states = mlstm(r, f, states)
    qk_tail[...] = nxt[1]
    u_tail[...] = nxt[2]
    for hd in range(HEADS):
        c_state[hd] = states[hd][0]
        m_state[hd:hd + 1, :] = states[hd][1]


def _layer_spec(stacked, layer, buffers=None):
    nd = stacked.ndim - 1
    kw = {} if buffers is None else dict(pipeline_mode=pl.Buffered(buffers))
    return pl.BlockSpec((None,) + stacked.shape[1:], lambda b, t: (layer,) + (0,) * nd, **kw)


def _whole_spec(a):
    return pl.BlockSpec(a.shape, lambda b, t: (0,) * a.ndim, pipeline_mode=pl.Buffered(1))


def _mixer(x, layer, gmix, w_in_t, bg, wconv, ghead, wpool, pscale, w_out, w_up, w_down):
    B, T, D = x.shape
    tm = MIX_TM
    steps = B * (T // tm)
    row_spec = pl.BlockSpec((1, tm, D), lambda b, t: (b, t, 0))
    params = (gmix, w_in_t, bg, wconv, ghead, wpool, pscale, w_out)
    big = (w_in_t, w_out)

    def param_spec(a):
        if a.ndim == 2:
            return pl.BlockSpec(a.shape, lambda b, t: (0, 0))
        return _layer_spec(a, layer, 1 if any(a is w for w in big) else None)

    def slab(w):
        return w.shape[1] // steps

    def cast_spec(w):
        return pl.BlockSpec((None, slab(w), w.shape[2]), lambda b, t: (layer, b * (T // tm) + t, 0))

    def cast_out_spec(w):
        return pl.BlockSpec((slab(w), w.shape[2]), lambda b, t: (b * (T // tm) + t, 0))

    return pl.pallas_call(
        functools.partial(_mixer_kernel, layer=layer),
        grid=(B, T // tm),
        in_specs=[row_spec] + [param_spec(a) for a in params] + [cast_spec(w_up), cast_spec(w_down)],
        out_specs=[row_spec, cast_out_spec(w_up), cast_out_spec(w_down)],
        out_shape=[jax.ShapeDtypeStruct(x.shape, x.dtype), jax.ShapeDtypeStruct(w_up.shape[1:], BF16),
                   jax.ShapeDtypeStruct(w_down.shape[1:], BF16)],
        scratch_shapes=[
            pltpu.VMEM((CONV_TAIL, 2 * QK_WIDTH), F32),
            pltpu.VMEM((POOL_TAIL, POOL_WIDTH), F32),
            pltpu.VMEM((HEADS, DK, 2 * DV), F32),
            pltpu.VMEM((SUBLANES, LANES), F32),
            pltpu.VMEM((LANES, tm), F32),
            pltpu.VMEM((D, MAIN_WIDTH + LANES), BF16),
            pltpu.VMEM(w_out.shape[1:], BF16),
            pltpu.VMEM((len(POOL_WINDOWS) // 2, 2 * POOL_GROUP, 2 * POOL_GROUP), BF16),
        ],
        compiler_params=pltpu.CompilerParams(
            dimension_semantics=("arbitrary", "arbitrary"), vmem_limit_bytes=VMEM_LIMIT),
        name="mixer",
    )(x, *params, w_up, w_down)


def _ffn_kernel(x_ref, gffn_ref, wup_ref, wconv_ref, bconv_ref, wdown_ref, gfinal_ref, out_ref,
                up_tail, act_s, hb_s, perm_s, *, layer, final_norm):
    tm, ngrp = FFN_TM, FFN_TM // PERM_GROUP
    t = pl.program_id(1)

    @pl.when(t == 0)
    def _():
        up_tail[...] = jnp.zeros_like(up_tail)

    def permuted(i):
        g, s = divmod(i, SUBLANES)
        return pl.ds(g * PERM_GROUP + s, SUBLANES, stride=SUBLANES)

    ntile = D_MODEL // LANES
    lanes_of = lambda c: slice(c * LANES, (c + 1) * LANES)
    row = slice(layer, layer + 1)
    h = _rms(x_ref[0], gffn_ref[row, :])
    for i in range(tm // SUBLANES):
        for c in range(ntile):
            perm_s[c, permuted(i), :] = h[i * SUBLANES:(i + 1) * SUBLANES, lanes_of(c)]
    hb_s[...] = jnp.concatenate([perm_s[c] for c in range(ntile)], axis=1).astype(BF16)

    last_sublane = lax.broadcasted_iota(jnp.int32, (ngrp, SUBLANES, FFN_SUB), 1) == SUBLANES - 1

    def conv(cols, scale=None):
        up = _dot(hb_s[...], wup_ref[:, cols])
        up4 = up.reshape(ngrp, SUBLANES, SUBLANES, FFN_SUB)
        tail = up_tail[:, cols].reshape(FFN_CONV - 1, SUBLANES, FFN_SUB)
        up_tail[:, cols] = up[tm - (FFN_CONV - 1) * SUBLANES:, :]
        wc, bias = wconv_ref[:, cols], bconv_ref[row, cols]
        if scale is not None:
            wc, bias = wc * scale, bias * scale
        wrapped = []
        for k in range(FFN_CONV - 1):
            v = SUBLANES - (FFN_CONV - 1) + k
            here = up4[:, v]
            before = jnp.concatenate([tail[k:k + 1], here[:-1]], axis=0)
            mixed = jnp.where(last_sublane, before, here)
            wrapped.append(jnp.stack([pltpu.roll(mixed[g], 1, axis=0) for g in range(ngrp)])[:, None])
        acc = up * wc[FFN_CONV - 1:FFN_CONV, :] + bias
        for j in range(1, FFN_CONV):
            delayed = jnp.concatenate(wrapped[FFN_CONV - 1 - j:] + [up4[:, :SUBLANES - j]], axis=1)
            acc = acc + delayed.reshape(tm, FFN_SUB) * wc[FFN_CONV - 1 - j:FFN_CONV - j, :]
        return acc

    for f in range(D_FF // FFN_SUB):
        half_gate = conv(slice(f * FFN_SUB, (f + 1) * FFN_SUB), 0.5)
        val = conv(slice(D_FF + f * FFN_SUB, D_FF + (f + 1) * FFN_SUB))
        act = half_gate * (1.0 + jnp.tanh(half_gate)) * val
        act_s[:, f * FFN_SUB:(f + 1) * FFN_SUB] = act.astype(BF16)

    down = _dot(act_s[...], wdown_ref[...])
    for c in range(ntile):
        perm_s[c] = down[:, lanes_of(c)]
    for i in range(tm // SUBLANES):
        rows = slice(i * SUBLANES, (i + 1) * SUBLANES)
        y = x_ref[0, rows, :] + jnp.concatenate([perm_s[c, permuted(i), :] for c in range(ntile)], axis=1)
        if final_norm:
            y = _rms(y, gfinal_ref[...])
        out_ref[0, rows, :] = y


def _ffn(x, layer, gffn, wup, wconv, bconv, wdown, gfinal, final_norm):
    B, T, D = x.shape
    tm = FFN_TM
    row_spec = pl.BlockSpec((1, tm, D), lambda b, t: (b, t, 0))
    small = lambda a: pl.BlockSpec(a.shape, lambda b, t: (0, 0))
    return pl.pallas_call(
        functools.partial(_ffn_kernel, layer=layer, final_norm=final_norm),
        grid=(B, T // tm),
        in_specs=[row_spec, small(gffn), _whole_spec(wup), _layer_spec(wconv, layer),
                  small(bconv), _whole_spec(wdown), small(gfinal)],
        out_specs=row_spec,
        out_shape=jax.ShapeDtypeStruct(x.shape, x.dtype),
        scratch_shapes=[
            pltpu.VMEM(((FFN_CONV - 1) * SUBLANES, 2 * D_FF), F32),
            pltpu.VMEM((tm, D_FF), BF16),
            pltpu.VMEM((tm, D), BF16),
            pltpu.VMEM((D // LANES, tm, LANES), F32),
        ],
        compiler_params=pltpu.CompilerParams(
            dimension_semantics=("arbitrary", "arbitrary"), vmem_limit_bytes=VMEM_LIMIT),
        name="ffn",
    )(x, gffn, wup, wconv, bconv, wdown, gfinal)


def kernel(x, mix_norm, w_in, b_gates, w_qk_conv, head_norm, w_pool, pool_scale, w_out, ffn_norm, w_up,
           w_ffn_conv, b_ffn_conv, w_down, final_norm):
    depth = w_in.shape[0]
    bg = jnp.pad(b_gates, ((0, 0), (0, LANES - GATES)))
    w_in_t = jnp.swapaxes(w_in, 1, 2)
    gfinal = final_norm.reshape(1, -1)
    for l in range(depth):
        x, wup, wdown = _mixer(x, l, mix_norm, w_in_t, bg, w_qk_conv, head_norm, w_pool, pool_scale, w_out,
                               w_up, w_down)
        x = _ffn(x, l, ffn_norm, wup, w_ffn_conv, b_ffn_conv, wdown, gfinal, final_norm=(l == depth - 1))
    return x
```

```python
import functools

import jax
import jax.numpy as jnp
from jax import lax
from jax.experimental import pallas as pl
from jax.experimental.pallas import tpu as pltpu

D_MODEL = 1024
HEADS = 4
DV = 128
DK = 64
QK_WIDTH = HEADS * DK
V_WIDTH = HEADS * DV
QK_CONV = 4
POOL_WINDOWS = (2, 4, 8, 16)
POOL_GROUP = 128
POOL_WIDTH = len(POOL_WINDOWS) * POOL_GROUP
GATES = 2 * HEADS
GATE_LO = 2 * QK_WIDTH + 2 * V_WIDTH
MAIN_WIDTH = GATE_LO + POOL_WIDTH
D_FF = 2816
FFN_CONV = 3
EPS = 1e-6

LANES = 128
SUBLANES = 8
MXU_COLS = 256

MIX_TM = 1024
MIX_GROUP = 256
MIX_CHUNK = 128
FFN_TM = 1024
FFN_SUB = 4 * MXU_COLS
POOL_TAIL = 16
CONV_TAIL = SUBLANES
VMEM_LIMIT = 56 * 1024 * 1024

F32 = jnp.float32
BF16 = jnp.bfloat16


def _dot(a, b):
    return jnp.dot(a, b, preferred_element_type=F32)


def _rms(x, g):
    return x * lax.rsqrt(jnp.mean(x * x, axis=-1, keepdims=True) + EPS) * g


def _silu(x):
    return x * jax.nn.sigmoid(x)


def _shift_rows(ext, j, tail):
    return pltpu.roll(ext, j, axis=0)[tail:, :]


def _mixer_kernel(x_ref, gmix_ref, win_t_ref, bg_ref, wconv_ref, ghead_ref, wpool_f32_ref,
                  pscale_ref, wout_f32_ref, wup_f32_ref, wdown_f32_ref, out_ref, wup_ref, wdown_ref,
                  qk_tail, u_tail, c_state, m_state, zt_s, wmain_ref, wout_ref, wpool_ref, *, layer):
    wup_ref[...] = wup_f32_ref[...].astype(BF16)
    wdown_ref[...] = wdown_f32_ref[...].astype(BF16)

    tm, grp, chunk = MIX_TM, MIX_GROUP, MIX_CHUNK
    ngrp, cpg = tm // grp, grp // chunk
    t = pl.program_id(1)

    @pl.when((pl.program_id(0) == 0) & (t == 0))
    def _():
        for dst in range(0, MAIN_WIDTH, MIX_GROUP):
            src = dst if dst < GATE_LO else dst + GATES
            wmain_ref[:, dst:dst + MIX_GROUP] = win_t_ref[src:src + MIX_GROUP, :].T.astype(BF16)
        gates_w = jnp.concatenate([win_t_ref[GATE_LO:GATE_LO + GATES, :],
                                   jnp.zeros((LANES - GATES, D_MODEL), F32)], axis=0)
        wmain_ref[:, MAIN_WIDTH:] = gates_w.T.astype(BF16)
        for rows in (slice(r, r + MIX_GROUP) for r in range(0, D_MODEL, MIX_GROUP)):
            wout_ref[rows, :] = wout_f32_ref[rows, :].astype(BF16)
        wpool_ref[...] = jnp.zeros_like(wpool_ref)
        for g in range(len(POOL_WINDOWS)):
            lo = (g % 2) * POOL_GROUP
            wpool_ref[g // 2, lo:lo + POOL_GROUP, lo:lo + POOL_GROUP] = wpool_f32_ref[g].astype(BF16)

    @pl.when(t == 0)
    def _():
        qk_tail[...] = jnp.zeros_like(qk_tail)
        u_tail[...] = jnp.zeros_like(u_tail)
        c_state[...] = jnp.zeros_like(c_state)
        m_state[...] = jnp.zeros_like(m_state)

    lane = lax.broadcasted_iota(jnp.int32, (chunk, LANES), 1)
    is_f = (lane >= HEADS) & (lane < GATES)
    causal = (lax.broadcasted_iota(jnp.int32, (chunk, chunk), 0)
              >= lax.broadcasted_iota(jnp.int32, (chunk, chunk), 1))
    tri = causal.astype(BF16)
    wc = wconv_ref[...]
    row = slice(layer, layer + 1)
    pscale = pscale_ref[row, :]
    gmix = gmix_ref[row, :]
    ghead = ghead_ref[row, :]
    bg = bg_ref[row, :]
    ones_blk = jnp.ones((chunk, DV), BF16)
    k_pad = jnp.zeros((chunk, MXU_COLS - chunk - DK), BF16)
    rhs_pad = jnp.zeros((MXU_COLS - chunk - DK, 2 * DV), BF16)
    rows_of = lambda c: slice(c * chunk, (c + 1) * chunk)

    def front(r, qk_hist, u_hist):
        rows = slice(r * grp, (r + 1) * grp)
        hb = _rms(x_ref[0, rows, :], gmix).astype(BF16)
        p = _dot(hb, wmain_ref[...])
        gates = p[:, MAIN_WIDTH:] + bg

        qk_pre = p[:, :2 * QK_WIDTH]
        ext = jnp.concatenate([qk_hist, qk_pre], axis=0)
        acc = qk_pre * wc[QK_CONV - 1:QK_CONV, :]
        for j in range(1, QK_CONV):
            acc = acc + _shift_rows(ext, j, CONV_TAIL) * wc[QK_CONV - 1 - j:QK_CONV - j, :]
        qk = _silu(acc)
        f = dict(q=qk[:, :QK_WIDTH].astype(BF16),
                 kt=(qk[:, QK_WIDTH:] * (DK ** -0.5)).T,
                 v=p[:, 2 * QK_WIDTH:2 * QK_WIDTH + V_WIDTH].astype(BF16),
                 o=p[:, 2 * QK_WIDTH + V_WIDTH:2 * QK_WIDTH + 2 * V_WIDTH])

        z_parts = []
        for c in range(cpg):
            g_c = gates[rows_of(c), :]
            lf = jnp.where(is_f, jnp.minimum(g_c, 0.0) - jnp.log(1.0 + jnp.exp(-jnp.abs(g_c))), 0.0)
            hi = lf.astype(BF16)
            r1 = lf - hi.astype(F32)
            mid = r1.astype(BF16)
            lo = (r1 - mid.astype(F32)).astype(BF16)
            b3 = _dot(tri, jnp.concatenate([hi, mid, lo], axis=1))
            b_c = b3[:, :LANES] + b3[:, LANES:2 * LANES] + b3[:, 2 * LANES:]
            z_parts.append(jnp.where(is_f, b_c, g_c))
        f["z"] = jnp.concatenate(z_parts, axis=0)
        zt_s[:, rows] = f["z"].T
        f["g_r"] = {}
        for c in range(cpg):
            lanes = slice(r * grp + c * chunk, r * grp + (c + 1) * chunk)
            for hd in range(HEADS):
                f["g_r"][c, hd] = zt_s[hd:hd + 1, lanes] - zt_s[HEADS + hd:HEADS + hd + 1, lanes]

        u = p[:, 2 * QK_WIDTH + 2 * V_WIDTH:MAIN_WIDTH]
        sums = jnp.concatenate([u_hist, u], axis=0)
        pos = (t * tm + r * grp + 1 + lax.broadcasted_iota(jnp.int32, (grp, 1), 0)).astype(F32)
        ds = []
        for g, w in enumerate(POOL_WINDOWS):
            sums = sums + pltpu.roll(sums, w // 2, axis=0)
            mean = sums[POOL_TAIL:, :POOL_GROUP] * (1.0 / jnp.minimum(pos, float(w)))
            ds.append((mean - u[:, g * POOL_GROUP:(g + 1) * POOL_GROUP]).astype(BF16))
            sums = sums[:, POOL_GROUP:]
        ys = []
        for g2 in range(len(POOL_WINDOWS) // 2):
            cols = slice(2 * g2 * POOL_GROUP, 2 * (g2 + 1) * POOL_GROUP)
            y = _dot(jnp.concatenate(ds[2 * g2:2 * g2 + 2], axis=1), wpool_ref[g2]) * pscale[:, cols]
            ys.append(y.astype(BF16))
        f["pool_out"] = _dot(jnp.concatenate(ys, axis=1), wout_ref[V_WIDTH:, :])
        return f, qk_pre[grp - CONV_TAIL:, :], u[grp - POOL_TAIL:, :]

    def mlstm(r, f, states):
        pairs = [(c, hd) for c in range(cpg) for hd in range(HEADS)]

        st = {}
        for c, hd in pairs:
            g_r = f["g_r"][c, hd]
            bc = jnp.broadcast_to(f["z"][rows_of(c), HEADS + hd:HEADS + hd + 1], (chunk, LANES))
            log_d = jnp.where(causal, bc + g_r, -jnp.inf)
            m_intra = jnp.broadcast_to(jnp.max(log_d, axis=1, keepdims=True), (chunk, LANES))
            g_max = jnp.broadcast_to(jnp.max(g_r, axis=1, keepdims=True), (1, LANES))
            st[c, hd] = dict(g_r=g_r, bc=bc, log_d=log_d, m_intra=m_intra, g_max=g_max)

        for c, hd in pairs:
            d = st[c, hd]
            q_c = f["q"][rows_of(c), hd * DK:(hd + 1) * DK]
            kt_c = f["kt"][hd * DK:(hd + 1) * DK, rows_of(c)]
            dmat = jnp.exp(d["log_d"] - d["m_intra"])
            d["s"] = _dot(q_c, kt_c.astype(BF16)) * dmat
            d["ktw"] = (kt_c * jnp.exp(d["g_r"] - d["g_max"])).astype(BF16)
            d["v_aug"] = jnp.concatenate([f["v"][rows_of(c), hd * DV:(hd + 1) * DV], ones_blk], axis=1)
            d["q_c"] = q_c

        for c, hd in pairs:
            d = st[c, hd]
            d["c_loc"] = _dot(d["ktw"], d["v_aug"])

        new_states = []
        for hd in range(HEADS):
            c_aug, m_prev = states[hd]
            for c in range(cpg):
                d = st[c, hd]
                d["c_prev"], d["m_prev"] = c_aug.astype(BF16), m_prev
                mx = jnp.maximum(m_prev, d["g_max"])
                s_prev, s_loc = jnp.exp(m_prev - mx), jnp.exp(d["g_max"] - mx)
                c_aug = (jnp.concatenate([s_prev, s_prev], axis=1) * c_aug
                         + jnp.concatenate([s_loc, s_loc], axis=1) * d["c_loc"])
                m_prev = d["bc"][chunk - 1:chunk, :] + mx
            new_states.append((c_aug, m_prev))

        for c, hd in pairs:
            d = st[c, hd]
            m_inter = d["bc"] + d["m_prev"]
            m = jnp.maximum(m_inter, d["m_intra"])
            s_w = (jnp.exp(d["m_intra"] - m) * d["s"]).astype(BF16)
            q_w = (jnp.exp(m_inter - m)[:, :DK] * d["q_c"].astype(F32)).astype(BF16)
            lhs = jnp.concatenate([s_w, q_w, k_pad], axis=1)
            rhs = jnp.concatenate([d["v_aug"], d["c_prev"], rhs_pad], axis=0)
            d["tot"], d["m"] = _dot(lhs, rhs), m

        for c in range(cpg):
            ys = []
            for hd in range(HEADS):
                d = st[c, hd]
                den = d["tot"][:, DV:]
                hh = d["tot"][:, :DV] * (1.0 / jnp.maximum(jnp.abs(den), jnp.exp(-d["m"])))
                cols = slice(hd * DV, (hd + 1) * DV)
                y = hh * lax.rsqrt(jnp.mean(hh * hh, axis=-1, keepdims=True) + EPS) * ghead[:, cols]
                ys.append((y * jax.nn.sigmoid(f["o"][rows_of(c), cols])).astype(BF16))
            rows = slice(r * grp + c * chunk, r * grp + (c + 1) * chunk)
            mixed = _dot(jnp.concatenate(ys, axis=1), wout_ref[:V_WIDTH, :]) + f["pool_out"][rows_of(c), :]
            out_ref[0, rows, :] = x_ref[0, rows, :] + mixed
        return new_states

    states = [(c_state[hd], m_state[hd:hd + 1, :]) for hd in range(HEADS)]
    nxt = front(0, qk_tail[...], u_tail[...])
    for r in range(ngrp):
        f, qk_hist, u_hist = nxt
        if r + 1 < ngrp:
            nxt = front(r + 1, qk_hist, u_hist)
        states = mlstm(r, f, states)
    qk_tail[...] = nxt[1]
    u_tail[...] = nxt[2]
    for hd in range(HEADS):
        c_state[hd] = states[hd][0]
        m_state[hd:hd + 1, :] = states[hd][1]


def _layer_spec(stacked, layer, buffers=None):
    nd = stacked.ndim - 1
    kw = {} if buffers is None else dict(pipeline_mode=pl.Buffered(buffers))
    return pl.BlockSpec((None,) + stacked.shape[1:], lambda b, t: (layer,) + (0,) * nd, **kw)


def _whole_spec(a):
    return pl.BlockSpec(a.shape, lambda b, t: (0,) * a.ndim, pipeline_mode=pl.Buffered(1))


def _mixer(x, layer, gmix, w_in_t, bg, wconv, ghead, wpool, pscale, w_out, w_up, w_down):
    B, T, D = x.shape
    tm = MIX_TM
    steps = B * (T // tm)
    row_spec = pl.BlockSpec((1, tm, D), lambda b, t: (b, t, 0))
    params = (gmix, w_in_t, bg, wconv, ghead, wpool, pscale, w_out)
    big = (w_in_t, w_out)

    def param_spec(a):
        if a.ndim == 2:
            return pl.BlockSpec(a.shape, lambda b, t: (0, 0))
        return _layer_spec(a, layer, 1 if any(a is w for w in big) else None)

    def slab(w):
        return w.shape[1] // steps

    def cast_spec(w):
        return pl.BlockSpec((None, slab(w), w.shape[2]), lambda b, t: (layer, b * (T // tm) + t, 0))

    def cast_out_spec(w):
        return pl.BlockSpec((slab(w), w.shape[2]), lambda b, t: (b * (T // tm) + t, 0))

    return pl.pallas_call(
        functools.partial(_mixer_kernel, layer=layer),
        grid=(B, T // tm),
        in_specs=[row_spec] + [param_spec(a) for a in params] + [cast_spec(w_up), cast_spec(w_down)],
        out_specs=[row_spec, cast_out_spec(w_up), cast_out_spec(w_down)],
        out_shape=[jax.ShapeDtypeStruct(x.shape, x.dtype), jax.ShapeDtypeStruct(w_up.shape[1:], BF16),
                   jax.ShapeDtypeStruct(w_down.shape[1:], BF16)],
        scratch_shapes=[
            pltpu.VMEM((CONV_TAIL, 2 * QK_WIDTH), F32),
            pltpu.VMEM((POOL_TAIL, POOL_WIDTH), F32),
            pltpu.VMEM((HEADS, DK, 2 * DV), F32),
            pltpu.VMEM((SUBLANES, LANES), F32),
            pltpu.VMEM((LANES, tm), F32),
            pltpu.VMEM((D, MAIN_WIDTH + LANES), BF16),
            pltpu.VMEM(w_out.shape[1:], BF16),
            pltpu.VMEM((len(POOL_WINDOWS) // 2, 2 * POOL_GROUP, 2 * POOL_GROUP), BF16),
        ],
        compiler_params=pltpu.CompilerParams(
            dimension_semantics=("arbitrary", "arbitrary"), vmem_limit_bytes=VMEM_LIMIT),
        name="mixer",
    )(x, *params, w_up, w_down)


def _ffn_kernel(x_ref, gffn_ref, wup_ref, wconv_ref, bconv_ref, wdown_ref, gfinal_ref, out_ref,
                up_tail, act_s, hb_s, *, layer, final_norm):
    tm = FFN_TM
    t = pl.program_id(1)

    @pl.when(t == 0)
    def _():
        up_tail[...] = jnp.zeros_like(up_tail)

    row = slice(layer, layer + 1)
    hb_s[...] = _rms(x_ref[0], gffn_ref[row, :]).astype(BF16)

    def conv(cols, scale=None):
        up = _dot(hb_s[...], wup_ref[:, cols])
        ext = jnp.concatenate([up_tail[:, cols], up], axis=0)
        up_tail[:, cols] = up[tm - CONV_TAIL:, :]
        wc, bias = wconv_ref[:, cols], bconv_ref[row, cols]
        if scale is not None:
            wc, bias = wc * scale, bias * scale
        acc = up * wc[FFN_CONV - 1:FFN_CONV, :] + bias
        for j in range(1, FFN_CONV):
            acc = acc + _shift_rows(ext, j, CONV_TAIL) * wc[FFN_CONV - 1 - j:FFN_CONV - j, :]
        return acc

    for lo in range(0, D_FF, FFN_SUB):
        hi = min(lo + FFN_SUB, D_FF)
        half_gate = conv(slice(lo, hi), 0.5)
        val = conv(slice(D_FF + lo, D_FF + hi))
        act = half_gate * (1.0 + jnp.tanh(half_gate)) * val
        act_s[:, lo:hi] = act.astype(BF16)

    y = x_ref[0] + _dot(act_s[...], wdown_ref[...])
    if final_norm:
        y = _rms(y, gfinal_ref[...])
    out_ref[0] = y


def _ffn(x, layer, gffn, wup, wconv, bconv, wdown, gfinal, final_norm):
    B, T, D = x.shape
    tm = FFN_TM
    row_spec = pl.BlockSpec((1, tm, D), lambda b, t: (b, t, 0))
    small = lambda a: pl.BlockSpec(a.shape, lambda b, t: (0, 0))
    return pl.pallas_call(
        functools.partial(_ffn_kernel, layer=layer, final_norm=final_norm),
        grid=(B, T // tm),
        in_specs=[row_spec, small(gffn), _whole_spec(wup), _layer_spec(wconv, layer),
                  small(bconv), _whole_spec(wdown), small(gfinal)],
        out_specs=row_spec,
        out_shape=jax.ShapeDtypeStruct(x.shape, x.dtype),
        scratch_shapes=[
            pltpu.VMEM((CONV_TAIL, 2 * D_FF), F32),
            pltpu.VMEM((tm, D_FF), BF16),
            pltpu.VMEM((tm, D), BF16),
        ],
        compiler_params=pltpu.CompilerParams(
            dimension_semantics=("arbitrary", "arbitrary"), vmem_limit_bytes=VMEM_LIMIT),
        name="ffn",
    )(x, gffn, wup, wconv, bconv, wdown, gfinal)


def kernel(x, mix_norm, w_in, b_gates, w_qk_conv, head_norm, w_pool, pool_scale, w_out, ffn_norm, w_up,
           w_ffn_conv, b_ffn_conv, w_down, final_norm):
    depth = w_in.shape[0]
    bg = jnp.pad(b_gates, ((0, 0), (0, LANES - GATES)))
    w_in_t = jnp.swapaxes(w_in, 1, 2)
    gfinal = final_norm.reshape(1, -1)
    for l in range(depth):
        x, wup, wdown = _mixer(x, l, mix_norm, w_in_t, bg, w_qk_conv, head_norm, w_pool, pool_scale, w_out,
                               w_up, w_down)
        x = _ffn(x, l, ffn_norm, wup, w_ffn_conv, b_ffn_conv, wdown, gfinal, final_norm=(l == depth - 1))
    return x
```

```python
import functools

import jax
import jax.numpy as jnp
from jax import lax
from jax.experimental import pallas as pl
from jax.experimental.pallas import tpu as pltpu

D_MODEL = 1024
HEADS = 4
DV = 128
DK = 64
QK_WIDTH = HEADS * DK
V_WIDTH = HEADS * DV
QK_CONV = 4
POOL_WINDOWS = (2, 4, 8, 16)
POOL_GROUP = 128
POOL_WIDTH = len(POOL_WINDOWS) * POOL_GROUP
GATES = 2 * HEADS
GATE_LO = 2 * QK_WIDTH + 2 * V_WIDTH
MAIN_WIDTH = GATE_LO + POOL_WIDTH
D_FF = 2816
FFN_CONV = 3
EPS = 1e-6

LANES = 128
SUBLANES = 8
MXU_COLS = 256

MIX_TM = 1024
MIX_GROUP = 256
MIX_CHUNK = 128
FFN_TM = 1024
FFN_SUB = 6 * MXU_COLS
POOL_TAIL = 16
CONV_TAIL = SUBLANES
VMEM_LIMIT = 56 * 1024 * 1024

F32 = jnp.float32
BF16 = jnp.bfloat16


def _dot(a, b):
    return jnp.dot(a, b, preferred_element_type=F32)


def _rms(x, g):
    return x * lax.rsqrt(jnp.mean(x * x, axis=-1, keepdims=True) + EPS) * g


def _silu(x):
    return x * jax.nn.sigmoid(x)


def _shift_rows(ext, j, tail):
    return pltpu.roll(ext, j, axis=0)[tail:, :]


def _mixer_kernel(x_ref, gmix_ref, win_t_ref, bg_ref, wconv_ref, ghead_ref, wpool_f32_ref,
                  pscale_ref, wout_f32_ref, wup_f32_ref, wdown_f32_ref, out_ref, wup_ref, wdown_ref,
                  qk_tail, u_tail, c_state, m_state, zt_s, wmain_ref, wout_ref, wpool_ref, *, layer):
    wup_ref[...] = wup_f32_ref[...].astype(BF16)
    wdown_ref[...] = wdown_f32_ref[...].astype(BF16)

    tm, grp, chunk = MIX_TM, MIX_GROUP, MIX_CHUNK
    ngrp, cpg = tm // grp, grp // chunk
    t = pl.program_id(1)

    @pl.when((pl.program_id(0) == 0) & (t == 0))
    def _():
        for dst in range(0, MAIN_WIDTH, MIX_GROUP):
            src = dst if dst < GATE_LO else dst + GATES
            wmain_ref[:, dst:dst + MIX_GROUP] = win_t_ref[src:src + MIX_GROUP, :].T.astype(BF16)
        gates_w = jnp.concatenate([win_t_ref[GATE_LO:GATE_LO + GATES, :],
                                   jnp.zeros((LANES - GATES, D_MODEL), F32)], axis=0)
        wmain_ref[:, MAIN_WIDTH:] = gates_w.T.astype(BF16)
        for rows in (slice(r, r + MIX_GROUP) for r in range(0, D_MODEL, MIX_GROUP)):
            wout_ref[rows, :] = wout_f32_ref[rows, :].astype(BF16)
        wpool_ref[...] = jnp.zeros_like(wpool_ref)
        for g in range(len(POOL_WINDOWS)):
            lo = (g % 2) * POOL_GROUP
            wpool_ref[g // 2, lo:lo + POOL_GROUP, lo:lo + POOL_GROUP] = wpool_f32_ref[g].astype(BF16)

    @pl.when(t == 0)
    def _():
        qk_tail[...] = jnp.zeros_like(qk_tail)
        u_tail[...] = jnp.zeros_like(u_tail)
        c_state[...] = jnp.zeros_like(c_state)
        m_state[...] = jnp.zeros_like(m_state)

    lane = lax.broadcasted_iota(jnp.int32, (chunk, LANES), 1)
    is_f = (lane >= HEADS) & (lane < GATES)
    causal = (lax.broadcasted_iota(jnp.int32, (chunk, chunk), 0)
              >= lax.broadcasted_iota(jnp.int32, (chunk, chunk), 1))
    tri = causal.astype(BF16)
    wc = wconv_ref[...]
    row = slice(layer, layer + 1)
    pscale = pscale_ref[row, :]
    gmix = gmix_ref[row, :]
    ghead = ghead_ref[row, :]
    bg = bg_ref[row, :]
    ones_blk = jnp.ones((chunk, DV), BF16)
    k_pad = jnp.zeros((chunk, MXU_COLS - chunk - DK), BF16)
    rhs_pad = jnp.zeros((MXU_COLS - chunk - DK, 2 * DV), BF16)
    rows_of = lambda c: slice(c * chunk, (c + 1) * chunk)

    def front(r, qk_hist, u_hist):
        rows = slice(r * grp, (r + 1) * grp)
        hb = _rms(x_ref[0, rows, :], gmix).astype(BF16)
        p = _dot(hb, wmain_ref[...])
        gates = p[:, MAIN_WIDTH:] + bg

        qk_pre = p[:, :2 * QK_WIDTH]
        ext = jnp.concatenate([qk_hist, qk_pre], axis=0)
        acc = qk_pre * wc[QK_CONV - 1:QK_CONV, :]
        for j in range(1, QK_CONV):
            acc = acc + _shift_rows(ext, j, CONV_TAIL) * wc[QK_CONV - 1 - j:QK_CONV - j, :]
        qk = _silu(acc)
        f = dict(q=qk[:, :QK_WIDTH].astype(BF16),
                 kt=(qk[:, QK_WIDTH:] * (DK ** -0.5)).T,
                 v=p[:, 2 * QK_WIDTH:2 * QK_WIDTH + V_WIDTH].astype(BF16),
                 o=p[:, 2 * QK_WIDTH + V_WIDTH:2 * QK_WIDTH + 2 * V_WIDTH])

        z_parts = []
        for c in range(cpg):
            g_c = gates[rows_of(c), :]
            lf = jnp.where(is_f, jnp.minimum(g_c, 0.0) - jnp.log(1.0 + jnp.exp(-jnp.abs(g_c))), 0.0)
            hi = lf.astype(BF16)
            r1 = lf - hi.astype(F32)
            mid = r1.astype(BF16)
            lo = (r1 - mid.astype(F32)).astype(BF16)
            b3 = _dot(tri, jnp.concatenate([hi, mid, lo], axis=1))
            b_c = b3[:, :LANES] + b3[:, LANES:2 * LANES] + b3[:, 2 * LANES:]
            z_parts.append(jnp.where(is_f, b_c, g_c))
        f["z"] = jnp.concatenate(z_parts, axis=0)
        zt_s[:, rows] = f["z"].T
        f["g_r"] = {}
        for c in range(cpg):
            lanes = slice(r * grp + c * chunk, r * grp + (c + 1) * chunk)
            for hd in range(HEADS):
                f["g_r"][c, hd] = zt_s[hd:hd + 1, lanes] - zt_s[HEADS + hd:HEADS + hd + 1, lanes]

        u = p[:, 2 * QK_WIDTH + 2 * V_WIDTH:MAIN_WIDTH]
        sums = jnp.concatenate([u_hist, u], axis=0)
        pos = (t * tm + r * grp + 1 + lax.broadcasted_iota(jnp.int32, (grp, 1), 0)).astype(F32)
        ds = []
        for g, w in enumerate(POOL_WINDOWS):
            sums = sums + pltpu.roll(sums, w // 2, axis=0)
            mean = sums[POOL_TAIL:, :POOL_GROUP] * (1.0 / jnp.minimum(pos, float(w)))
            ds.append((mean - u[:, g * POOL_GROUP:(g + 1) * POOL_GROUP]).astype(BF16))
            sums = sums[:, POOL_GROUP:]
        ys = []
        for g2 in range(len(POOL_WINDOWS) // 2):
            cols = slice(2 * g2 * POOL_GROUP, 2 * (g2 + 1) * POOL_GROUP)
            y = _dot(jnp.concatenate(ds[2 * g2:2 * g2 + 2], axis=1), wpool_ref[g2]) * pscale[:, cols]
            ys.append(y.astype(BF16))
        f["pool_out"] = _dot(jnp.concatenate(ys, axis=1), wout_ref[V_WIDTH:, :])
        return f, qk_pre[grp - CONV_TAIL:, :], u[grp - POOL_TAIL:, :]

    def mlstm(r, f, states):
        pairs = [(c, hd) for c in range(cpg) for hd in range(HEADS)]

        st = {}
        for c, hd in pairs:
            g_r = f["g_r"][c, hd]
            bc = jnp.broadcast_to(f["z"][rows_of(c), HEADS + hd:HEADS + hd + 1], (chunk, LANES))
            log_d = jnp.where(causal, bc + g_r, -jnp.inf)
            m_intra = jnp.broadcast_to(jnp.max(log_d, axis=1, keepdims=True), (chunk, LANES))
            g_max = jnp.broadcast_to(jnp.max(g_r, axis=1, keepdims=True), (1, LANES))
            st[c, hd] = dict(g_r=g_r, bc=bc, log_d=log_d, m_intra=m_intra, g_max=g_max)

        for c, hd in pairs:
            d = st[c, hd]
            q_c = f["q"][rows_of(c), hd * DK:(hd + 1) * DK]
            kt_c = f["kt"][hd * DK:(hd + 1) * DK, rows_of(c)]
            dmat = jnp.exp(d["log_d"] - d["m_intra"])
            d["s"] = _dot(q_c, kt_c.astype(BF16)) * dmat
            d["ktw"] = (kt_c * jnp.exp(d["g_r"] - d["g_max"])).astype(BF16)
            d["v_aug"] = jnp.concatenate([f["v"][rows_of(c), hd * DV:(hd + 1) * DV], ones_blk], axis=1)
            d["q_c"] = q_c

        for c, hd in pairs:
            d = st[c, hd]
            d["c_loc"] = _dot(d["ktw"], d["v_aug"])

        new_states = []
        for hd in range(HEADS):
            c_aug, m_prev = states[hd]
            for c in range(cpg):
                d = st[c, hd]
                d["c_prev"], d["m_prev"] = c_aug.astype(BF16), m_prev
                mx = jnp.maximum(m_prev, d["g_max"])
                s_prev, s_loc = jnp.exp(m_prev - mx), jnp.exp(d["g_max"] - mx)
                c_aug = (jnp.concatenate([s_prev, s_prev], axis=1) * c_aug
                         + jnp.concatenate([s_loc, s_loc], axis=1) * d["c_loc"])
                m_prev = d["bc"][chunk - 1:chunk, :] + mx
            new_states.append((c_aug, m_prev))

        for c, hd in pairs:
            d = st[c, hd]
            m_inter = d["bc"] + d["m_prev"]
            m = jnp.maximum(m_inter, d["m_intra"])
            s_w = (jnp.exp(d["m_intra"] - m) * d["s"]).astype(BF16)
            q_w = (jnp.exp(m_inter - m)[:, :DK] * d["q_c"].astype(F32)).astype(BF16)
            lhs = jnp.concatenate([s_w, q_w, k_pad], axis=1)
            rhs = jnp.concatenate([d["v_aug"], d["c_prev"], rhs_pad], axis=0)
            d["tot"], d["m"] = _dot(lhs, rhs), m

        for c in range(cpg):
            ys = []
            for hd in range(HEADS):
                d = st[c, hd]
                den = d["tot"][:, DV:]
                hh = d["tot"][:, :DV] * (1.0 / jnp.maximum(jnp.abs(den), jnp.exp(-d["m"])))
                cols = slice(hd * DV, (hd + 1) * DV)
                y = hh * lax.rsqrt(jnp.mean(hh * hh, axis=-1, keepdims=True) + EPS) * ghead[:, cols]
                ys.append((y * jax.nn.sigmoid(f["o"][rows_of(c), cols])).astype(BF16))
            rows = slice(r * grp + c * chunk, r * grp + (c + 1) * chunk)
            mixed = _dot(jnp.concatenate(ys, axis=1), wout_ref[:V_WIDTH, :]) + f["pool_out"][rows_of(c), :]
            out_ref[0, rows, :] = x_ref[0, rows, :] + mixed
        return new_states

    states = [(c_state[hd], m_state[hd:hd + 1, :]) for hd in range(HEADS)]
    nxt = front(0, qk_tail[...], u_tail[...])
    for r in range(ngrp):
        f, qk_hist, u_hist = nxt
        if r + 1 < ngrp:
            nxt = front(r + 1, qk_hist, u_hist)
        states = mlstm(r, f, states)
    qk_tail[...] = nxt[1]
    u_tail[...] = nxt[2]
    for hd in range(HEADS):
        c_state[hd] = states[hd][0]
        m_state[hd:hd + 1, :] = states[hd][1]


def _layer_spec(stacked, layer, buffers=None):
    nd = stacked.ndim - 1
    kw = {} if buffers is None else dict(pipeline_mode=pl.Buffered(buffers))
    return pl.BlockSpec((None,) + stacked.shape[1:], lambda b, t: (layer,) + (0,) * nd, **kw)


def _whole_spec(a):
    return pl.BlockSpec(a.shape, lambda b, t: (0,) * a.ndim, pipeline_mode=pl.Buffered(1))


def _mixer(x, layer, gmix, w_in_t, bg, wconv, ghead, wpool, pscale, w_out, w_up, w_down):
    B, T, D = x.shape
    tm = MIX_TM
    steps = B * (T // tm)
    row_spec = pl.BlockSpec((1, tm, D), lambda b, t: (b, t, 0))
    params = (gmix, w_in_t, bg, wconv, ghead, wpool, pscale, w_out)
    big = (w_in_t, w_out)

    def param_spec(a):
        if a.ndim == 2:
            return pl.BlockSpec(a.shape, lambda b, t: (0, 0))
        return _layer_spec(a, layer, 1 if any(a is w for w in big) else None)

    def slab(w):
        return w.shape[1] // steps

    def cast_spec(w):
        return pl.BlockSpec((None, slab(w), w.shape[2]), lambda b, t: (layer, b * (T // tm) + t, 0))

    def cast_out_spec(w):
        return pl.BlockSpec((slab(w), w.shape[2]), lambda b, t: (b * (T // tm) + t, 0))

    return pl.pallas_call(
        functools.partial(_mixer_kernel, layer=layer),
        grid=(B, T // tm),
        in_specs=[row_spec] + [param_spec(a) for a in params] + [cast_spec(w_up), cast_spec(w_down)],
        out_specs=[row_spec, cast_out_spec(w_up), cast_out_spec(w_down)],
        out_shape=[jax.ShapeDtypeStruct(x.shape, x.dtype), jax.ShapeDtypeStruct(w_up.shape[1:], BF16),
                   jax.ShapeDtypeStruct(w_down.shape[1:], BF16)],
        scratch_shapes=[
            pltpu.VMEM((CONV_TAIL, 2 * QK_WIDTH), F32),
            pltpu.VMEM((POOL_TAIL, POOL_WIDTH), F32),
            pltpu.VMEM((HEADS, DK, 2 * DV), F32),
            pltpu.VMEM((SUBLANES, LANES), F32),
            pltpu.VMEM((LANES, tm), F32),
            pltpu.VMEM((D, MAIN_WIDTH + LANES), BF16),
            pltpu.VMEM(w_out.shape[1:], BF16),
            pltpu.VMEM((len(POOL_WINDOWS) // 2, 2 * POOL_GROUP, 2 * POOL_GROUP), BF16),
        ],
        compiler_params=pltpu.CompilerParams(
            dimension_semantics=("arbitrary", "arbitrary"), vmem_limit_bytes=VMEM_LIMIT),
        name="mixer",
    )(x, *params, w_up, w_down)


def _ffn_kernel(x_ref, gffn_ref, wup_ref, wconv_ref, bconv_ref, wdown_ref, gfinal_ref, out_ref,
                up_tail, act_s, hb_s, *, layer, final_norm):
    tm = FFN_TM
    t = pl.program_id(1)

    @pl.when(t == 0)
    def _():
        up_tail[...] = jnp.zeros_like(up_tail)

    row = slice(layer, layer + 1)
    hb_s[...] = _rms(x_ref[0], gffn_ref[row, :]).astype(BF16)

    def conv(cols, scale=None):
        up = _dot(hb_s[...], wup_ref[:, cols])
        ext = jnp.concatenate([up_tail[:, cols], up], axis=0)
        up_tail[:, cols] = up[tm - CONV_TAIL:, :]
        wc, bias = wconv_ref[:, cols], bconv_ref[row, cols]
        if scale is not None:
            wc, bias = wc * scale, bias * scale
        acc = up * wc[FFN_CONV - 1:FFN_CONV, :] + bias
        for j in range(1, FFN_CONV):
            acc = acc + _shift_rows(ext, j, CONV_TAIL) * wc[FFN_CONV - 1 - j:FFN_CONV - j, :]
        return acc

    for lo in range(0, D_FF, FFN_SUB):
        hi = min(lo + FFN_SUB, D_FF)
        half_gate = conv(slice(lo, hi), 0.5)
        val = conv(slice(D_FF + lo, D_FF + hi))
        act = half_gate * (1.0 + jnp.tanh(half_gate)) * val
        act_s[:, lo:hi] = act.astype(BF16)

    y = x_ref[0] + _dot(act_s[...], wdown_ref[...])
    if final_norm:
        y = _rms(y, gfinal_ref[...])
    out_ref[0] = y


def _ffn(x, layer, gffn, wup, wconv, bconv, wdown, gfinal, final_norm):
    B, T, D = x.shape
    tm = FFN_TM
    row_spec = pl.BlockSpec((1, tm, D), lambda b, t: (b, t, 0))
    small = lambda a: pl.BlockSpec(a.shape, lambda b, t: (0, 0))
    return pl.pallas_call(
        functools.partial(_ffn_kernel, layer=layer, final_norm=final_norm),
        grid=(B, T // tm),
        in_specs=[row_spec, small(gffn), _whole_spec(wup), _layer_spec(wconv, layer),
                  small(bconv), _whole_spec(wdown), small(gfinal)],
        out_specs=row_spec,
        out_shape=jax.ShapeDtypeStruct(x.shape, x.dtype),
        scratch_shapes=[
            pltpu.VMEM((CONV_TAIL, 2 * D_FF), F32),
            pltpu.VMEM((tm, D_FF), BF16),
            pltpu.VMEM((tm, D), BF16),
        ],
        compiler_params=pltpu.CompilerParams(
            dimension_semantics=("arbitrary", "arbitrary"), vmem_limit_bytes=VMEM_LIMIT),
        name="ffn",
    )(x, gffn, wup, wconv, bconv, wdown, gfinal)


def kernel(x, mix_norm, w_in, b_gates, w_qk_conv, head_norm, w_pool, pool_scale, w_out, ffn_norm, w_up,
           w_ffn_conv, b_ffn_conv, w_down, final_norm):
    depth = w_in.shape[0]
    bg = jnp.pad(b_gates, ((0, 0), (0, LANES - GATES)))
    w_in_t = jnp.swapaxes(w_in, 1, 2)
    gfinal = final_norm.reshape(1, -1)
    for l in range(depth):
        x, wup, wdown = _mixer(x, l, mix_norm, w_in_t, bg, w_qk_conv, head_norm, w_pool, pool_scale, w_out,
                               w_up, w_down)
        x = _ffn(x, l, ffn_norm, wup, w_ffn_conv, b_ffn_conv, wdown, gfinal, final_norm=(l == depth - 1))
    return x
```

```python
import functools

import jax
import jax.numpy as jnp
from jax import lax
from jax.experimental import pallas as pl
from jax.experimental.pallas import tpu as pltpu

D_MODEL = 1024
HEADS = 4
DV = 128
DK = 64
QK_WIDTH = HEADS * DK
V_WIDTH = HEADS * DV
QK_CONV = 4
POOL_WINDOWS = (2, 4, 8, 16)
POOL_GROUP = 128
POOL_WIDTH = len(POOL_WINDOWS) * POOL_GROUP
GATES = 2 * HEADS
GATE_LO = 2 * QK_WIDTH + 2 * V_WIDTH
MAIN_WIDTH = GATE_LO + POOL_WIDTH
D_FF = 2816
FFN_CONV = 3
EPS = 1e-6

LANES = 128
SUBLANES = 8
MXU_COLS = 256

MIX_TM = 1024
MIX_GROUP = 256
MIX_CHUNK = 128
FFN_TM = 1024
FFN_SUB = 6 * MXU_COLS
POOL_TAIL = 16
CONV_TAIL = SUBLANES
VMEM_LIMIT = 56 * 1024 * 1024

F32 = jnp.float32
BF16 = jnp.bfloat16


def _dot(a, b):
    return jnp.dot(a, b, preferred_element_type=F32)


def _rms(x, g):
    return x * lax.rsqrt(jnp.mean(x * x, axis=-1, keepdims=True) + EPS) * g


def _silu(x):
    return x * jax.nn.sigmoid(x)


def _shift_rows(ext, j, tail):
    return pltpu.roll(ext, j, axis=0)[tail:, :]


def _mixer_kernel(x_ref, gmix_ref, win_t_ref, bg_ref, wconv_ref, ghead_ref, wpool_f32_ref,
                  pscale_ref, wout_f32_ref, wup_f32_ref, wdown_f32_ref, out_ref, wup_ref, wdown_ref,
                  qk_tail, u_tail, c_state, m_state, zt_s, wmain_ref, wout_ref, wpool_ref, *, layer):
    wup_ref[...] = wup_f32_ref[...].astype(BF16)
    wdown_ref[...] = wdown_f32_ref[...].astype(BF16)

    tm, grp, chunk = MIX_TM, MIX_GROUP, MIX_CHUNK
    ngrp, cpg = tm // grp, grp // chunk
    t = pl.program_id(1)

    @pl.when((pl.program_id(0) == 0) & (t == 0))
    def _():
        for dst in range(0, MAIN_WIDTH, MIX_GROUP):
            src = dst if dst < GATE_LO else dst + GATES
            wmain_ref[:, dst:dst + MIX_GROUP] = win_t_ref[src:src + MIX_GROUP, :].T.astype(BF16)
        gates_w = jnp.concatenate([win_t_ref[GATE_LO:GATE_LO + GATES, :],
                                   jnp.zeros((LANES - GATES, D_MODEL), F32)], axis=0)
        wmain_ref[:, MAIN_WIDTH:] = gates_w.T.astype(BF16)
        for rows in (slice(r, r + MIX_GROUP) for r in range(0, D_MODEL, MIX_GROUP)):
            wout_ref[rows, :] = wout_f32_ref[rows, :].astype(BF16)
        wpool_ref[...] = jnp.zeros_like(wpool_ref)
        for g in range(len(POOL_WINDOWS)):
            lo = (g % 2) * POOL_GROUP
            wpool_ref[g // 2, lo:lo + POOL_GROUP, lo:lo + POOL_GROUP] = wpool_f32_ref[g].astype(BF16)

    @pl.when(t == 0)
    def _():
        qk_tail[...] = jnp.zeros_like(qk_tail)
        u_tail[...] = jnp.zeros_like(u_tail)
        c_state[...] = jnp.zeros_like(c_state)
        m_state[...] = jnp.zeros_like(m_state)

    lane = lax.broadcasted_iota(jnp.int32, (chunk, LANES), 1)
    is_f = (lane >= HEADS) & (lane < GATES)
    causal = (lax.broadcasted_iota(jnp.int32, (chunk, chunk), 0)
              >= lax.broadcasted_iota(jnp.int32, (chunk, chunk), 1))
    tri = causal.astype(BF16)
    wc = wconv_ref[...]
    row = slice(layer, layer + 1)
    pscale = pscale_ref[row, :]
    gmix = gmix_ref[row, :]
    ghead = ghead_ref[row, :]
    bg = bg_ref[row, :]
    ones_blk = jnp.ones((chunk, DV), BF16)
    k_pad = jnp.zeros((chunk, MXU_COLS - chunk - DK), BF16)
    rhs_pad = jnp.zeros((MXU_COLS - chunk - DK, 2 * DV), BF16)
    rows_of = lambda c: slice(c * chunk, (c + 1) * chunk)

    def front(r, qk_hist, u_hist):
        rows = slice(r * grp, (r + 1) * grp)
        hb = _rms(x_ref[0, rows, :], gmix).astype(BF16)
        p = _dot(hb, wmain_ref[...])
        gates = p[:, MAIN_WIDTH:] + bg

        qk_pre = p[:, :2 * QK_WIDTH]
        ext = jnp.concatenate([qk_hist, qk_pre], axis=0)
        acc = qk_pre * wc[QK_CONV - 1:QK_CONV, :]
        for j in range(1, QK_CONV):
            acc = acc + _shift_rows(ext, j, CONV_TAIL) * wc[QK_CONV - 1 - j:QK_CONV - j, :]
        qk = _silu(acc)
        f = dict(q=qk[:, :QK_WIDTH].astype(BF16),
                 kt=(qk[:, QK_WIDTH:] * (DK ** -0.5)).T,
                 v=p[:, 2 * QK_WIDTH:2 * QK_WIDTH + V_WIDTH].astype(BF16),
                 o=p[:, 2 * QK_WIDTH + V_WIDTH:2 * QK_WIDTH + 2 * V_WIDTH])

        z_parts = []
        for c in range(cpg):
            g_c = gates[rows_of(c), :]
            lf = jnp.where(is_f, jnp.minimum(g_c, 0.0) - jnp.log(1.0 + jnp.exp(-jnp.abs(g_c))), 0.0)
            hi = lf.astype(BF16)
            r1 = lf - hi.astype(F32)
            mid = r1.astype(BF16)
            lo = (r1 - mid.astype(F32)).astype(BF16)
            b3 = _dot(tri, jnp.concatenate([hi, mid, lo], axis=1))
            b_c = b3[:, :LANES] + b3[:, LANES:2 * LANES] + b3[:, 2 * LANES:]
            z_parts.append(jnp.where(is_f, b_c, g_c))
        f["z"] = jnp.concatenate(z_parts, axis=0)
        zt_s[:, rows] = f["z"].T
        f["g_r"] = {}
        for c in range(cpg):
            lanes = slice(r * grp + c * chunk, r * grp + (c + 1) * chunk)
            for hd in range(HEADS):
                f["g_r"][c, hd] = zt_s[hd:hd + 1, lanes] - zt_s[HEADS + hd:HEADS + hd + 1, lanes]

        u = p[:, 2 * QK_WIDTH + 2 * V_WIDTH:MAIN_WIDTH]
        sums = jnp.concatenate([u_hist, u], axis=0)
        pos = (t * tm + r * grp + 1 + lax.broadcasted_iota(jnp.int32, (grp, 1), 0)).astype(F32)
        ds = []
        for g, w in enumerate(POOL_WINDOWS):
            sums = sums + pltpu.roll(sums, w // 2, axis=0)
            mean = sums[POOL_TAIL:, :POOL_GROUP] * (1.0 / jnp.minimum(pos, float(w)))
            ds.append((mean - u[:, g * POOL_GROUP:(g + 1) * POOL_GROUP]).astype(BF16))
            sums = sums[:, POOL_GROUP:]
        ys = []
        for g2 in range(len(POOL_WINDOWS) // 2):
            cols = slice(2 * g2 * POOL_GROUP, 2 * (g2 + 1) * POOL_GROUP)
            y = _dot(jnp.concatenate(ds[2 * g2:2 * g2 + 2], axis=1), wpool_ref[g2]) * pscale[:, cols]
            ys.append(y.astype(BF16))
        f["pool_out"] = _dot(jnp.concatenate(ys, axis=1), wout_ref[V_WIDTH:, :])
        return f, qk_pre[grp - CONV_TAIL:, :], u[grp - POOL_TAIL:, :]

    def mlstm(r, f, states):
        pairs = [(c, hd) for c in range(cpg) for hd in range(HEADS)]

        st = {}
        for c, hd in pairs:
            g_r = f["g_r"][c, hd]
            bc = jnp.broadcast_to(f["z"][rows_of(c), HEADS + hd:HEADS + hd + 1], (chunk, LANES))
            log_d = jnp.where(causal, bc + g_r, -jnp.inf)
            m_intra = jnp.broadcast_to(jnp.max(log_d, axis=1, keepdims=True), (chunk, LANES))
            g_max = jnp.broadcast_to(jnp.max(g_r, axis=1, keepdims=True), (1, LANES))
            st[c, hd] = dict(g_r=g_r, bc=bc, log_d=log_d, m_intra=m_intra, g_max=g_max)

        for c, hd in pairs:
            d = st[c, hd]
            q_c = f["q"][rows_of(c), hd * DK:(hd + 1) * DK]
            kt_c = f["kt"][hd * DK:(hd + 1) * DK, rows_of(c)]
            dmat = jnp.exp(d["log_d"] - d["m_intra"])
            d["s"] = _dot(q_c, kt_c.astype(BF16)) * dmat
            d["ktw"] = (kt_c * jnp.exp(d["g_r"] - d["g_max"])).astype(BF16)
            d["v_aug"] = jnp.concatenate([f["v"][rows_of(c), hd * DV:(hd + 1) * DV], ones_blk], axis=1)
            d["q_c"] = q_c

        for c, hd in pairs:
            d = st[c, hd]
            d["c_loc"] = _dot(d["ktw"], d["v_aug"])

        new_states = []
        for hd in range(HEADS):
            c_aug, m_prev = states[hd]
            for c in range(cpg):
                d = st[c, hd]
                d["c_prev"], d["m_prev"] = c_aug.astype(BF16), m_prev
                mx = jnp.maximum(m_prev, d["g_max"])
                s_prev, s_loc = jnp.exp(m_prev - mx), jnp.exp(d["g_max"] - mx)
                c_aug = (jnp.concatenate([s_prev, s_prev], axis=1) * c_aug
                         + jnp.concatenate([s_loc, s_loc], axis=1) * d["c_loc"])
                m_prev = d["bc"][chunk - 1:chunk, :] + mx
            new_states.append((c_aug, m_prev))

        for c, hd in pairs:
            d = st[c, hd]
            m_inter = d["bc"] + d["m_prev"]
            m = jnp.maximum(m_inter, d["m_intra"])
            s_w = (jnp.exp(d["m_intra"] - m) * d["s"]).astype(BF16)
            q_w = (jnp.exp(m_inter - m)[:, :DK] * d["q_c"].astype(F32)).astype(BF16)
            lhs = jnp.concatenate([s_w, q_w, k_pad], axis=1)
            rhs = jnp.concatenate([d["v_aug"], d["c_prev"], rhs_pad], axis=0)
            d["tot"], d["m"] = _dot(lhs, rhs), m

        for c in range(cpg):
            ys = []
            for hd in range(HEADS):
                d = st[c, hd]
                den = d["tot"][:, DV:]
                hh = d["tot"][:, :DV] * (1.0 / jnp.maximum(jnp.abs(den), jnp.exp(-d["m"])))
                cols = slice(hd * DV, (hd + 1) * DV)
                y = hh * lax.rsqrt(jnp.mean(hh * hh, axis=-1, keepdims=True) + EPS) * ghead[:, cols]
                ys.append((y * jax.nn.sigmoid(f["o"][rows_of(c), cols])).astype(BF16))
            rows = slice(r * grp + c * chunk, r * grp + (c + 1) * chunk)
            mixed = _dot(jnp.concatenate(ys, axis=1), wout_ref[:V_WIDTH, :]) + f["pool_out"][rows_of(c), :]
            out_ref[0, rows, :] = x_ref[0, rows, :] + mixed
        return new_states

    states = [(c_state[hd], m_state[hd:hd + 1, :]) for hd in range(HEADS)]
    nxt = front(0, qk_tail[...], u_tail[...])
    for r in range(ngrp):
        f, qk_hist, u_hist = nxt
        if r + 1 < ngrp:
            nxt = front(r + 1, qk_hist, u_hist)
        states = mlstm(r, f, states)
    qk_tail[...] = nxt[1]
    u_tail[...] = nxt[2]
    for hd in range(HEADS):
        c_state[hd] = states[hd][0]
        m_state[hd:hd + 1, :] = states[hd][1]


def _layer_spec(stacked, layer, buffers=None):
    nd = stacked.ndim - 1
    kw = {} if buffers is None else dict(pipeline_mode=pl.Buffered(buffers))
    return pl.BlockSpec((None,) + stacked.shape[1:], lambda b, t: (layer,) + (0,) * nd, **kw)


def _whole_spec(a):
    return pl.BlockSpec(a.shape, lambda b, t: (0,) * a.ndim, pipeline_mode=pl.Buffered(1))


def _mixer(x, layer, gmix, w_in_t, bg, wconv, ghead, wpool, pscale, w_out, w_up, w_down):
    B, T, D = x.shape
    tm = MIX_TM
    steps = B * (T // tm)
    row_spec = pl.BlockSpec((1, tm, D), lambda b, t: (b, t, 0))
    params = (gmix, w_in_t, bg, wconv, ghead, wpool, pscale, w_out)
    big = (w_in_t, w_out)

    def param_spec(a):
        if a.ndim == 2:
            return pl.BlockSpec(a.shape, lambda b, t: (0, 0))
        return _layer_spec(a, layer, 1 if any(a is w for w in big) else None)

    def slab(w):
        return w.shape[1] // steps

    def cast_spec(w):
        return pl.BlockSpec((None, slab(w), w.shape[2]), lambda b, t: (layer, b * (T // tm) + t, 0))

    def cast_out_spec(w):
        return pl.BlockSpec((slab(w), w.shape[2]), lambda b, t: (b * (T // tm) + t, 0))

    return pl.pallas_call(
        functools.partial(_mixer_kernel, layer=layer),
        grid=(B, T // tm),
        in_specs=[row_spec] + [param_spec(a) for a in params] + [cast_spec(w_up), cast_spec(w_down)],
        out_specs=[row_spec, cast_out_spec(w_up), cast_out_spec(w_down)],
        out_shape=[jax.ShapeDtypeStruct(x.shape, x.dtype), jax.ShapeDtypeStruct(w_up.shape[1:], BF16),
                   jax.ShapeDtypeStruct(w_down.shape[1:], BF16)],
        scratch_shapes=[
            pltpu.VMEM((CONV_TAIL, 2 * QK_WIDTH), F32),
            pltpu.VMEM((POOL_TAIL, POOL_WIDTH), F32),
            pltpu.VMEM((HEADS, DK, 2 * DV), F32),
            pltpu.VMEM((SUBLANES, LANES), F32),
            pltpu.VMEM((LANES, tm), F32),
            pltpu.VMEM((D, MAIN_WIDTH + LANES), BF16),
            pltpu.VMEM(w_out.shape[1:], BF16),
            pltpu.VMEM((len(POOL_WINDOWS) // 2, 2 * POOL_GROUP, 2 * POOL_GROUP), BF16),
        ],
        compiler_params=pltpu.CompilerParams(
            dimension_semantics=("arbitrary", "arbitrary"), vmem_limit_bytes=VMEM_LIMIT),
        name="mixer",
    )(x, *params, w_up, w_down)


def _ffn_kernel(x_ref, gffn_ref, wup_ref, wconv_ref, bconv_ref, wdown_hbm, gfinal_ref, out_ref,
                up_tail, act_s, hb_s, wdown_ref, wdown_sem, *, layer, final_norm):
    tm = FFN_TM
    t = pl.program_id(1)
    first_step = (pl.program_id(0) == 0) & (t == 0)

    def wdown_copy():
        return pltpu.make_async_copy(wdown_hbm, wdown_ref, wdown_sem)

    @pl.when(first_step)
    def _():
        wdown_copy().start()

    @pl.when(t == 0)
    def _():
        up_tail[...] = jnp.zeros_like(up_tail)

    row = slice(layer, layer + 1)
    hb_s[...] = _rms(x_ref[0], gffn_ref[row, :]).astype(BF16)

    def conv(cols, scale=None):
        up = _dot(hb_s[...], wup_ref[:, cols])
        ext = jnp.concatenate([up_tail[:, cols], up], axis=0)
        up_tail[:, cols] = up[tm - CONV_TAIL:, :]
        wc, bias = wconv_ref[:, cols], bconv_ref[row, cols]
        if scale is not None:
            wc, bias = wc * scale, bias * scale
        acc = up * wc[FFN_CONV - 1:FFN_CONV, :] + bias
        for j in range(1, FFN_CONV):
            acc = acc + _shift_rows(ext, j, CONV_TAIL) * wc[FFN_CONV - 1 - j:FFN_CONV - j, :]
        return acc

    for lo in range(0, D_FF, FFN_SUB):
        hi = min(lo + FFN_SUB, D_FF)
        half_gate = conv(slice(lo, hi), 0.5)
        val = conv(slice(D_FF + lo, D_FF + hi))
        act = half_gate * (1.0 + jnp.tanh(half_gate)) * val
        act_s[:, lo:hi] = act.astype(BF16)

    @pl.when(first_step)
    def _():
        wdown_copy().wait()

    y = x_ref[0] + _dot(act_s[...], wdown_ref[...])
    if final_norm:
        y = _rms(y, gfinal_ref[...])
    out_ref[0] = y


def _ffn(x, layer, gffn, wup, wconv, bconv, wdown, gfinal, final_norm):
    B, T, D = x.shape
    tm = FFN_TM
    row_spec = pl.BlockSpec((1, tm, D), lambda b, t: (b, t, 0))
    small = lambda a: pl.BlockSpec(a.shape, lambda b, t: (0, 0))
    return pl.pallas_call(
        functools.partial(_ffn_kernel, layer=layer, final_norm=final_norm),
        grid=(B, T // tm),
        in_specs=[row_spec, small(gffn), _whole_spec(wup), _layer_spec(wconv, layer),
                  small(bconv), pl.BlockSpec(memory_space=pl.ANY), small(gfinal)],
        out_specs=row_spec,
        out_shape=jax.ShapeDtypeStruct(x.shape, x.dtype),
        scratch_shapes=[
            pltpu.VMEM((CONV_TAIL, 2 * D_FF), F32),
            pltpu.VMEM((tm, D_FF), BF16),
            pltpu.VMEM((tm, D), BF16),
            pltpu.VMEM(wdown.shape, BF16),
            pltpu.SemaphoreType.DMA(()),
        ],
        compiler_params=pltpu.CompilerParams(
            dimension_semantics=("arbitrary", "arbitrary"), vmem_limit_bytes=VMEM_LIMIT),
        name="ffn",
    )(x, gffn, wup, wconv, bconv, wdown, gfinal)


def kernel(x, mix_norm, w_in, b_gates, w_qk_conv, head_norm, w_pool, pool_scale, w_out, ffn_norm, w_up,
           w_ffn_conv, b_ffn_conv, w_down, final_norm):
    depth = w_in.shape[0]
    bg = jnp.pad(b_gates, ((0, 0), (0, LANES - GATES)))
    w_in_t = jnp.swapaxes(w_in, 1, 2)
    gfinal = final_norm.reshape(1, -1)
    for l in range(depth):
        x, wup, wdown = _mixer(x, l, mix_norm, w_in_t, bg, w_qk_conv, head_norm, w_pool, pool_scale, w_out,
                               w_up, w_down)
        x = _ffn(x, l, ffn_norm, wup, w_ffn_conv, b_ffn_conv, wdown, gfinal, final_norm=(l == depth - 1))
    return x
```

```python
import functools

import jax
import jax.numpy as jnp
from jax import lax
from jax.experimental import pallas as pl
from jax.experimental.pallas import tpu as pltpu

D_MODEL = 1024
HEADS = 4
DV = 128
DK = 64
QK_WIDTH = HEADS * DK
V_WIDTH = HEADS * DV
QK_CONV = 4
POOL_WINDOWS = (2, 4, 8, 16)
POOL_GROUP = 128
POOL_WIDTH = len(POOL_WINDOWS) * POOL_GROUP
GATES = 2 * HEADS
GATE_LO = 2 * QK_WIDTH + 2 * V_WIDTH
MAIN_WIDTH = GATE_LO + POOL_WIDTH
D_FF = 2816
FFN_CONV = 3
EPS = 1e-6

LANES = 128
SUBLANES = 8
MXU_COLS = 256

MIX_TM = 1024
MIX_GROUP = 256
MIX_CHUNK = 128
FFN_TM = 1024
FFN_SUB = 6 * MXU_COLS
POOL_TAIL = 16
CONV_TAIL = SUBLANES
VMEM_LIMIT = 56 * 1024 * 1024

F32 = jnp.float32
BF16 = jnp.bfloat16


def _dot(a, b):
    return jnp.dot(a, b, preferred_element_type=F32)


def _rms(x, g):
    return x * lax.rsqrt(jnp.mean(x * x, axis=-1, keepdims=True) + EPS) * g


def _silu(x):
    return x * jax.nn.sigmoid(x)


def _shift_rows(ext, j, tail):
    return pltpu.roll(ext, j, axis=0)[tail:, :]


def _mixer_kernel(x_ref, gmix_ref, win_t_ref, bg_ref, wconv_ref, ghead_ref, wpool_f32_ref,
                  pscale_ref, wout_f32_ref, wup_f32_ref, wdown_f32_ref, out_ref, wup_ref, wdown_ref,
                  qk_tail, u_tail, c_state, m_state, zt_s, wmain_ref, wout_ref, wpool_ref, *, layer):
    wup_ref[...] = wup_f32_ref[...].astype(BF16)
    wdown_ref[...] = wdown_f32_ref[...].astype(BF16)

    tm, grp, chunk = MIX_TM, MIX_GROUP, MIX_CHUNK
    ngrp, cpg = tm // grp, grp // chunk
    t = pl.program_id(1)

    @pl.when((pl.program_id(0) == 0) & (t == 0))
    def _():
        for dst in range(0, MAIN_WIDTH, MIX_GROUP):
            src = dst if dst < GATE_LO else dst + GATES
            wmain_ref[:, dst:dst + MIX_GROUP] = win_t_ref[src:src + MIX_GROUP, :].T.astype(BF16)
        gates_w = jnp.concatenate([win_t_ref[GATE_LO:GATE_LO + GATES, :],
                                   jnp.zeros((LANES - GATES, D_MODEL), F32)], axis=0)
        wmain_ref[:, MAIN_WIDTH:] = gates_w.T.astype(BF16)
        for rows in (slice(r, r + MIX_GROUP) for r in range(0, D_MODEL, MIX_GROUP)):
            wout_ref[rows, :] = wout_f32_ref[rows, :].astype(BF16)
        wpool_ref[...] = jnp.zeros_like(wpool_ref)
        for g in range(len(POOL_WINDOWS)):
            lo = (g % 2) * POOL_GROUP
            wpool_ref[g // 2, lo:lo + POOL_GROUP, lo:lo + POOL_GROUP] = wpool_f32_ref[g].astype(BF16)

    @pl.when(t == 0)
    def _():
        qk_tail[...] = jnp.zeros_like(qk_tail)
        u_tail[...] = jnp.zeros_like(u_tail)
        c_state[...] = jnp.zeros_like(c_state)
        m_state[...] = jnp.zeros_like(m_state)

    lane = lax.broadcasted_iota(jnp.int32, (chunk, LANES), 1)
    is_f = (lane >= HEADS) & (lane < GATES)
    causal = (lax.broadcasted_iota(jnp.int32, (chunk, chunk), 0)
              >= lax.broadcasted_iota(jnp.int32, (chunk, chunk), 1))
    tri = causal.astype(BF16)
    wc = wconv_ref[...]
    row = slice(layer, layer + 1)
    pscale = pscale_ref[row, :]
    gmix = gmix_ref[row, :]
    ghead = ghead_ref[row, :]
    bg = bg_ref[row, :]
    ones_blk = jnp.ones((chunk, DV), BF16)
    k_pad = jnp.zeros((chunk, MXU_COLS - chunk - DK), BF16)
    rhs_pad = jnp.zeros((MXU_COLS - chunk - DK, 2 * DV), BF16)
    rows_of = lambda c: slice(c * chunk, (c + 1) * chunk)

    def front(r, qk_hist, u_hist):
        rows = slice(r * grp, (r + 1) * grp)
        hb = _rms(x_ref[0, rows, :], gmix).astype(BF16)
        p = _dot(hb, wmain_ref[...])
        gates = p[:, MAIN_WIDTH:] + bg

        qk_pre = p[:, :2 * QK_WIDTH]
        ext = jnp.concatenate([qk_hist, qk_pre], axis=0)
        acc = qk_pre * wc[QK_CONV - 1:QK_CONV, :]
        for j in range(1, QK_CONV):
            acc = acc + _shift_rows(ext, j, CONV_TAIL) * wc[QK_CONV - 1 - j:QK_CONV - j, :]
        qk = _silu(acc)
        f = dict(q=qk[:, :QK_WIDTH].astype(BF16),
                 kt=(qk[:, QK_WIDTH:] * (DK ** -0.5)).T,
                 v=p[:, 2 * QK_WIDTH:2 * QK_WIDTH + V_WIDTH].astype(BF16),
                 o=p[:, 2 * QK_WIDTH + V_WIDTH:2 * QK_WIDTH + 2 * V_WIDTH])

        z_parts = []
        for c in range(cpg):
            g_c = gates[rows_of(c), :]
            lf = jnp.where(is_f, jnp.minimum(g_c, 0.0) - jnp.log(1.0 + jnp.exp(-jnp.abs(g_c))), 0.0)
            hi = lf.astype(BF16)
            r1 = lf - hi.astype(F32)
            mid = r1.astype(BF16)
            lo = (r1 - mid.astype(F32)).astype(BF16)
            b3 = _dot(tri, jnp.concatenate([hi, mid, lo], axis=1))
            b_c = b3[:, :LANES] + b3[:, LANES:2 * LANES] + b3[:, 2 * LANES:]
            z_parts.append(jnp.where(is_f, b_c, g_c))
        f["z"] = jnp.concatenate(z_parts, axis=0)
        zt_s[:, rows] = f["z"].T
        f["g_r"] = {}
        for c in range(cpg):
            lanes = slice(r * grp + c * chunk, r * grp + (c + 1) * chunk)
            for hd in range(HEADS):
                f["g_r"][c, hd] = zt_s[hd:hd + 1, lanes] - zt_s[HEADS + hd:HEADS + hd + 1, lanes]

        u = p[:, 2 * QK_WIDTH + 2 * V_WIDTH:MAIN_WIDTH]
        sums = jnp.concatenate([u_hist, u], axis=0)
        pos = (t * tm + r * grp + 1 + lax.broadcasted_iota(jnp.int32, (grp, 1), 0)).astype(F32)
        ds = []
        for g, w in enumerate(POOL_WINDOWS):
            sums = sums + pltpu.roll(sums, w // 2, axis=0)
            mean = sums[POOL_TAIL:, :POOL_GROUP] * (1.0 / jnp.minimum(pos, float(w)))
            ds.append((mean - u[:, g * POOL_GROUP:(g + 1) * POOL_GROUP]).astype(BF16))
            sums = sums[:, POOL_GROUP:]
        ys = []
        for g2 in range(len(POOL_WINDOWS) // 2):
            cols = slice(2 * g2 * POOL_GROUP, 2 * (g2 + 1) * POOL_GROUP)
            y = _dot(jnp.concatenate(ds[2 * g2:2 * g2 + 2], axis=1), wpool_ref[g2]) * pscale[:, cols]
            ys.append(y.astype(BF16))
        f["pool_out"] = _dot(jnp.concatenate(ys, axis=1), wout_ref[V_WIDTH:, :])
        return f, qk_pre[grp - CONV_TAIL:, :], u[grp - POOL_TAIL:, :]

    def mlstm(r, f, states):
        pairs = [(c, hd) for c in range(cpg) for hd in range(HEADS)]

        st = {}
        for c, hd in pairs:
            g_r = f["g_r"][c, hd]
            bc = jnp.broadcast_to(f["z"][rows_of(c), HEADS + hd:HEADS + hd + 1], (chunk, LANES))
            log_d = jnp.where(causal, bc + g_r, -jnp.inf)
            m_intra = jnp.broadcast_to(jnp.max(log_d, axis=1, keepdims=True), (chunk, LANES))
            g_max = jnp.broadcast_to(jnp.max(g_r, axis=1, keepdims=True), (1, LANES))
            st[c, hd] = dict(g_r=g_r, bc=bc, log_d=log_d, m_intra=m_intra, g_max=g_max)

        for c, hd in pairs:
            d = st[c, hd]
            q_c = f["q"][rows_of(c), hd * DK:(hd + 1) * DK]
            kt_c = f["kt"][hd * DK:(hd + 1) * DK, rows_of(c)]
            dmat = jnp.exp(d["log_d"] - d["m_intra"])
            d["s"] = _dot(q_c, kt_c.astype(BF16)) * dmat
            d["ktw"] = (kt_c * jnp.exp(d["g_r"] - d["g_max"])).astype(BF16)
            d["v_aug"] = jnp.concatenate([f["v"][rows_of(c), hd * DV:(hd + 1) * DV], ones_blk], axis=1)
            d["q_c"] = q_c

        for c, hd in pairs:
            d = st[c, hd]
            d["c_loc"] = _dot(d["ktw"], d["v_aug"])

        new_states = []
        for hd in range(HEADS):
            c_aug, m_prev = states[hd]
            for c in range(cpg):
                d = st[c, hd]
                d["c_prev"], d["m_prev"] = c_aug.astype(BF16), m_prev
                mx = jnp.maximum(m_prev, d["g_max"])
                s_prev, s_loc = jnp.exp(m_prev - mx), jnp.exp(d["g_max"] - mx)
                c_aug = (jnp.concatenate([s_prev, s_prev], axis=1) * c_aug
                         + jnp.concatenate([s_loc, s_loc], axis=1) * d["c_loc"])
                m_prev = d["bc"][chunk - 1:chunk, :] + mx
            new_states.append((c_aug, m_prev))

        for c, hd in pairs:
            d = st[c, hd]
            m_inter = d["bc"] + d["m_prev"]
            m = jnp.maximum(m_inter, d["m_intra"])
            s_w = (jnp.exp(d["m_intra"] - m) * d["s"]).astype(BF16)
            q_w = (jnp.exp(m_inter - m)[:, :DK] * d["q_c"].astype(F32)).astype(BF16)
            lhs = jnp.concatenate([s_w, q_w, k_pad], axis=1)
            rhs = jnp.concatenate([d["v_aug"], d["c_prev"], rhs_pad], axis=0)
            d["tot"], d["m"] = _dot(lhs, rhs), m

        for c in range(cpg):
            ys = []
            for hd in range(HEADS):
                d = st[c, hd]
                den = d["tot"][:, DV:]
                hh = d["tot"][:, :DV] * (1.0 / jnp.maximum(jnp.abs(den), jnp.exp(-d["m"])))
                cols = slice(hd * DV, (hd + 1) * DV)
                y = hh * lax.rsqrt(jnp.mean(hh * hh, axis=-1, keepdims=True) + EPS) * ghead[:, cols]
                ys.append((y * jax.nn.sigmoid(f["o"][rows_of(c), cols])).astype(BF16))
            rows = slice(r * grp + c * chunk, r * grp + (c + 1) * chunk)
            mixed = _dot(jnp.concatenate(ys, axis=1), wout_ref[:V_WIDTH, :]) + f["pool_out"][rows_of(c), :]
            out_ref[0, rows, :] = x_ref[0, rows, :] + mixed
        return new_states

    states = [(c_state[hd], m_state[hd:hd + 1, :]) for hd in range(HEADS)]
    nxt = front(0, qk_tail[...], u_tail[...])
    for r in range(ngrp):
        f, qk_hist, u_hist = nxt
        if r + 1 < ngrp:
            nxt = front(r + 1, qk_hist, u_hist)
        states = mlstm(r, f, states)
    qk_tail[...] = nxt[1]
    u_tail[...] = nxt[2]
    for hd in range(HEADS):
        c_state[hd] = states[hd][0]
        m_state[hd:hd + 1, :] = states[hd][1]


def _layer_spec(stacked, layer, buffers=None):
    nd = stacked.ndim - 1
    kw = {} if buffers is None else dict(pipeline_mode=pl.Buffered(buffers))
    return pl.BlockSpec((None,) + stacked.shape[1:], lambda b, t: (layer,) + (0,) * nd, **kw)


def _whole_spec(a):
    return pl.BlockSpec(a.shape, lambda b, t: (0,) * a.ndim, pipeline_mode=pl.Buffered(1))


def _mixer(x, layer, gmix, w_in_t, bg, wconv, ghead, wpool, pscale, w_out, w_up, w_down):
    B, T, D = x.shape
    tm = MIX_TM
    steps = B * (T // tm)
    row_spec = pl.BlockSpec((1, tm, D), lambda b, t: (b, t, 0))
    params = (gmix, w_in_t, bg, wconv, ghead, wpool, pscale, w_out)
    big = (w_in_t, w_out)

    def param_spec(a):
        if a.ndim == 2:
            return pl.BlockSpec(a.shape, lambda b, t: (0, 0))
        return _layer_spec(a, layer, 1 if any(a is w for w in big) else None)

    def slab(w):
        return w.shape[1] // steps

    def cast_spec(w):
        return pl.BlockSpec((None, slab(w), w.shape[2]), lambda b, t: (layer, b * (T // tm) + t, 0))

    def cast_out_spec(w):
        return pl.BlockSpec((slab(w), w.shape[2]), lambda b, t: (b * (T // tm) + t, 0))

    return pl.pallas_call(
        functools.partial(_mixer_kernel, layer=layer),
        grid=(B, T // tm),
        in_specs=[row_spec] + [param_spec(a) for a in params] + [cast_spec(w_up), cast_spec(w_down)],
        out_specs=[row_spec, cast_out_spec(w_up), cast_out_spec(w_down)],
        out_shape=[jax.ShapeDtypeStruct(x.shape, x.dtype), jax.ShapeDtypeStruct(w_up.shape[1:], BF16),
                   jax.ShapeDtypeStruct(w_down.shape[1:], BF16)],
        scratch_shapes=[
            pltpu.VMEM((CONV_TAIL, 2 * QK_WIDTH), F32),
            pltpu.VMEM((POOL_TAIL, POOL_WIDTH), F32),
            pltpu.VMEM((HEADS, DK, 2 * DV), F32),
            pltpu.VMEM((SUBLANES, LANES), F32),
            pltpu.VMEM((LANES, tm), F32),
            pltpu.VMEM((D, MAIN_WIDTH + LANES), BF16),
            pltpu.VMEM(w_out.shape[1:], BF16),
            pltpu.VMEM((len(POOL_WINDOWS) // 2, 2 * POOL_GROUP, 2 * POOL_GROUP), BF16),
        ],
        compiler_params=pltpu.CompilerParams(
            dimension_semantics=("arbitrary", "arbitrary"), vmem_limit_bytes=VMEM_LIMIT),
        name="mixer",
    )(x, *params, w_up, w_down)


def _ffn_kernel(x_ref, gffn_ref, wup_ref, wconv_ref, bconv_ref, wdown_ref, gfinal_ref, out_ref,
                up_tail, act_s, hb_s, *, layer, final_norm):
    tm = FFN_TM
    t = pl.program_id(1)

    @pl.when(t == 0)
    def _():
        up_tail[...] = jnp.zeros_like(up_tail)

    row = slice(layer, layer + 1)
    hb_s[...] = _rms(x_ref[0], gffn_ref[row, :]).astype(BF16)

    def conv(cols, scale=None):
        up = _dot(hb_s[...], wup_ref[:, cols])
        ext = jnp.concatenate([up_tail[:, cols], up], axis=0)
        up_tail[:, cols] = up[tm - CONV_TAIL:, :]
        wc, bias = wconv_ref[:, cols], bconv_ref[row, cols]
        if scale is not None:
            wc, bias = wc * scale, bias * scale
        acc = up * wc[FFN_CONV - 1:FFN_CONV, :] + bias
        for j in range(1, FFN_CONV):
            acc = acc + _shift_rows(ext, j, CONV_TAIL) * wc[FFN_CONV - 1 - j:FFN_CONV - j, :]
        return acc

    for lo in range(0, D_FF, FFN_SUB):
        hi = min(lo + FFN_SUB, D_FF)
        half_gate = conv(slice(lo, hi), 0.5)
        val = conv(slice(D_FF + lo, D_FF + hi))
        act = half_gate * (1.0 + jnp.tanh(half_gate)) * val
        act_s[:, lo:hi] = act.astype(BF16)

    for rows in (slice(0, tm // 2), slice(tm // 2, tm)):
        y = x_ref[0, rows, :] + _dot(act_s[rows, :], wdown_ref[...])
        if final_norm:
            y = _rms(y, gfinal_ref[...])
        out_ref[0, rows, :] = y


def _ffn(x, layer, gffn, wup, wconv, bconv, wdown, gfinal, final_norm):
    B, T, D = x.shape
    tm = FFN_TM
    row_spec = pl.BlockSpec((1, tm, D), lambda b, t: (b, t, 0))
    small = lambda a: pl.BlockSpec(a.shape, lambda b, t: (0, 0))
    return pl.pallas_call(
        functools.partial(_ffn_kernel, layer=layer, final_norm=final_norm),
        grid=(B, T // tm),
        in_specs=[row_spec, small(gffn), _whole_spec(wup), _layer_spec(wconv, layer),
                  small(bconv), _whole_spec(wdown), small(gfinal)],
        out_specs=row_spec,
        out_shape=jax.ShapeDtypeStruct(x.shape, x.dtype),
        scratch_shapes=[
            pltpu.VMEM((CONV_TAIL, 2 * D_FF), F32),
            pltpu.VMEM((tm, D_FF), BF16),
            pltpu.VMEM((tm, D), BF16),
        ],
        compiler_params=pltpu.CompilerParams(
            dimension_semantics=("arbitrary", "arbitrary"), vmem_limit_bytes=VMEM_LIMIT),
        name="ffn",
    )(x, gffn, wup, wconv, bconv, wdown, gfinal)


def kernel(x, mix_norm, w_in, b_gates, w_qk_conv, head_norm, w_pool, pool_scale, w_out, ffn_norm, w_up,
           w_ffn_conv, b_ffn_conv, w_down, final_norm):
    depth = w_in.shape[0]
    bg = jnp.pad(b_gates, ((0, 0), (0, LANES - GATES)))
    w_in_t = jnp.swapaxes(w_in, 1, 2)
    gfinal = final_norm.reshape(1, -1)
    for l in range(depth):
        x, wup, wdown = _mixer(x, l, mix_norm, w_in_t, bg, w_qk_conv, head_norm, w_pool, pool_scale, w_out,
                               w_up, w_down)
        x = _ffn(x, l, ffn_norm, wup, w_ffn_conv, b_ffn_conv, wdown, gfinal, final_norm=(l == depth - 1))
    return x
```
